```python
import jax, jax.numpy as jnp
from jax import lax
import numpy as np

D_MODEL = 2048
BATCH = 16
SEQ = 2048
DEPTH = 1

N_Q_HEADS = 8
N_KV_HEADS = 4
HEAD_DIM = 128
GQA_GROUP = N_Q_HEADS // N_KV_HEADS
ROPE_THETA = 10000.0
AXIS_ROPE_DIM = HEAD_DIM // 2
Q_BLOCK = 128
GRID_W = 64
CONV_CH = 1024
CONV_K = 31
FFN_DIM = 5632
FFN_CONV_K = 3
EPS = 1e-6

ATTN_Q = N_Q_HEADS * HEAD_DIM
ATTN_KV = N_KV_HEADS * HEAD_DIM
IN_COLS = ATTN_Q + 2 * ATTN_KV + 2 * CONV_CH + 2 * D_MODEL
SPLITS = (ATTN_Q, ATTN_Q + ATTN_KV, ATTN_Q + 2 * ATTN_KV, ATTN_Q + 2 * ATTN_KV + 2 * CONV_CH)

kernel_name = 'hybrid_gqa_axialrope_conformer_convffn_encoder'


def rmsnorm(x, g):
    xf = x.astype(jnp.float32)
    y = xf * lax.rsqrt(jnp.mean(xf * xf, axis=-1, keepdims=True) + EPS)
    return (y * g.astype(jnp.float32)).astype(x.dtype)


def layernorm(x, g, b):
    xf = x.astype(jnp.float32)
    mu = jnp.mean(xf, axis=-1, keepdims=True)
    var = jnp.mean(jnp.square(xf - mu), axis=-1, keepdims=True)
    y = (xf - mu) * lax.rsqrt(var + EPS)
    return (y * g.astype(jnp.float32) + b.astype(jnp.float32)).astype(x.dtype)


def depthwise_conv(x, w, b):
    k = w.shape[0]
    pad = k // 2
    y = lax.conv_general_dilated(
        x, w[:, None, :].astype(x.dtype), window_strides=(1,),
        padding=[(pad, pad)], dimension_numbers=('NWC', 'WIO', 'NWC'),
        feature_group_count=x.shape[-1])
    return y + b.astype(x.dtype)


def axial_rope_tables(seq_len):
    rows = seq_len // GRID_W
    row = jnp.repeat(jnp.arange(rows), GRID_W).astype(jnp.float32)
    col = jnp.tile(jnp.arange(GRID_W), rows).astype(jnp.float32)
    inv_freq = ROPE_THETA ** (-jnp.arange(0, AXIS_ROPE_DIM, 2, dtype=jnp.float32) / AXIS_ROPE_DIM)
    ang = jnp.concatenate([row[:, None] * inv_freq, col[:, None] * inv_freq], axis=-1)
    return jnp.cos(ang), jnp.sin(ang)


def apply_rope(x, cos, sin):
    xf = x.astype(jnp.float32).reshape(x.shape[:-1] + (HEAD_DIM // 2, 2))
    x0, x1 = xf[..., 0], xf[..., 1]
    c = cos[None, :, None, :]
    s = sin[None, :, None, :]
    out = jnp.stack([x0 * c - x1 * s, x0 * s + x1 * c], axis=-1).reshape(x.shape)
    return out.astype(x.dtype)


def block_attention(q, k, v):
    b, s = q.shape[0], q.shape[1]
    nb = s // Q_BLOCK
    qb = q.reshape(b, nb, Q_BLOCK, N_KV_HEADS, GQA_GROUP, HEAD_DIM).transpose(1, 0, 2, 3, 4, 5)
    scale = HEAD_DIM ** -0.5

    def one_block(qi):
        sc = jnp.einsum('bqkgd,bskd->bkgqs', qi, k, preferred_element_type=jnp.float32) * scale
        p = jax.nn.softmax(sc, axis=-1)
        return jnp.einsum('bkgqs,bskd->bqkgd', p.astype(v.dtype), v)

    o = lax.map(one_block, qb)
    return o.transpose(1, 0, 2, 3, 4, 5).reshape(b, s, ATTN_Q)


def setup_inputs(seed: int = 0) -> dict:
    key = jax.random.key(seed)
    ks = jax.random.split(key, 20)
    f32 = jnp.float32

    def nrm(k, shape, scale):
        return jax.random.normal(k, shape, f32) * scale

    L = DEPTH
    return {
        'x': jax.random.normal(ks[0], (BATCH, SEQ, D_MODEL), f32),
        'norm_mix_g': 1.0 + nrm(ks[1], (L, D_MODEL), 0.02),
        'w_in': nrm(ks[2], (L, D_MODEL, IN_COLS), D_MODEL ** -0.5),
        'b_in': nrm(ks[3], (L, IN_COLS), 0.01),
        'q_norm_g': 1.0 + nrm(ks[4], (L, HEAD_DIM), 0.02),
        'k_norm_g': 1.0 + nrm(ks[5], (L, HEAD_DIM), 0.02),
        'w_attn_o': nrm(ks[6], (L, ATTN_Q, D_MODEL), ATTN_Q ** -0.5),
        'conv_dw_w': nrm(ks[7], (L, CONV_K, CONV_CH), CONV_K ** -0.5),
        'conv_dw_b': nrm(ks[8], (L, CONV_CH), 0.01),
        'conv_ln_g': 1.0 + nrm(ks[9], (L, CONV_CH), 0.02),
        'conv_ln_b': nrm(ks[10], (L, CONV_CH), 0.01),
        'w_conv_o': nrm(ks[11], (L, CONV_CH, D_MODEL), CONV_CH ** -0.5),
        'w_out': nrm(ks[12], (L, D_MODEL, D_MODEL), D_MODEL ** -0.5),
        'norm_ffn_g': 1.0 + nrm(ks[13], (L, D_MODEL), 0.02),
        'w_ffn_up': nrm(ks[14], (L, D_MODEL, 2 * FFN_DIM), D_MODEL ** -0.5),
        'ffn_dw_w': nrm(ks[15], (L, FFN_CONV_K, FFN_DIM), FFN_CONV_K ** -0.5),
        'ffn_dw_b': nrm(ks[16], (L, FFN_DIM), 0.01),
        'w_ffn_down': nrm(ks[17], (L, FFN_DIM, D_MODEL), FFN_DIM ** -0.5),
        'norm_final_g': 1.0 + nrm(ks[18], (D_MODEL,), 0.02),
    }


def reference(x, norm_mix_g, w_in, b_in, q_norm_g, k_norm_g, w_attn_o, conv_dw_w, conv_dw_b,
              conv_ln_g, conv_ln_b, w_conv_o, w_out, norm_ffn_g, w_ffn_up, ffn_dw_w, ffn_dw_b,
              w_ffn_down, norm_final_g):
    b, s, _ = x.shape
    cos, sin = axial_rope_tables(s)
    for l in range(DEPTH):
        h = rmsnorm(x, norm_mix_g[l])
        z = h @ w_in[l] + b_in[l]
        q, k, v, cv, gates = jnp.split(z, SPLITS, axis=-1)

        q = apply_rope(rmsnorm(q.reshape(b, s, N_Q_HEADS, HEAD_DIM), q_norm_g[l]), cos, sin)
        k = apply_rope(rmsnorm(k.reshape(b, s, N_KV_HEADS, HEAD_DIM), k_norm_g[l]), cos, sin)
        v = v.reshape(b, s, N_KV_HEADS, HEAD_DIM)
        a = block_attention(q, k, v) @ w_attn_o[l]

        u, ug = jnp.split(cv, 2, axis=-1)
        c = u * jax.nn.sigmoid(ug)
        c = depthwise_conv(c, conv_dw_w[l], conv_dw_b[l])
        c = jax.nn.silu(layernorm(c, conv_ln_g[l], conv_ln_b[l]))
        c = c @ w_conv_o[l]

        g_a, g_c = jnp.split(jax.nn.sigmoid(gates), 2, axis=-1)
        x = x + (g_a * a + g_c * c) @ w_out[l]

        h2 = rmsnorm(x, norm_ffn_g[l])
        gt, val = jnp.split(h2 @ w_ffn_up[l], 2, axis=-1)
        gt = depthwise_conv(gt, ffn_dw_w[l], ffn_dw_b[l])
        x = x + (jax.nn.silu(gt) * val) @ w_ffn_down[l]
    return rmsnorm(x, norm_final_g)
```

```python
import functools

import numpy as np
import jax
import jax.numpy as jnp
from jax import lax
from jax.experimental import pallas as pl
from jax.experimental.pallas import tpu as pltpu

F32 = jnp.float32
BF16 = jnp.bfloat16

EPS = 1e-6
HEAD_DIM = 128
N_Q_HEADS = 8
N_KV_HEADS = 4
GQA_GROUP = N_Q_HEADS // N_KV_HEADS
ROPE_THETA = 10000.0
GRID_W = 64
CONV_CH = 1024

V7X_VMEM_BYTES = 64 * 1024 * 1024
VMEM_LIMIT = V7X_VMEM_BYTES - 8 * 1024 * 1024
SUBLANES = 8
LANES = 128
BF16_SUBLANES = 16

TM_PROJ = 1024
TN_PROJ = 512
TQ_ATTN = 512
TS_CONV = 512
CONV_UNROLL = 8
LN_ROWS = 128
TM_FFN = 512
TF_FFN = 512
HALO = BF16_SUBLANES


def _params(*sem):
    return pltpu.CompilerParams(dimension_semantics=sem, vmem_limit_bytes=VMEM_LIMIT)


def _rms_rows(x, g):
    ms = jnp.mean(x * x, axis=-1, keepdims=True)
    return x * lax.rsqrt(ms + EPS) * g


def _sigmoid(x):
    return 1.0 / (1.0 + jnp.exp(-x))


def _qkv_kernel(x_ref, g_ref, w_ref, b_ref, qg_ref, kg_ref, cs_ref, sn_ref, o_ref, h_ref):
    j = pl.program_id(1)

    @pl.when(j == 0)
    def _():
        h_ref[...] = _rms_rows(x_ref[...], g_ref[...]).astype(BF16)

    z = jnp.dot(h_ref[...], w_ref[...], preferred_element_type=F32) + b_ref[...]
    heads_per_tile = z.shape[1] // HEAD_DIM
    n_q_tiles = N_Q_HEADS // heads_per_tile
    n_k_tiles = N_KV_HEADS // heads_per_tile

    @pl.when(j < n_q_tiles + n_k_tiles)
    def _():
        gain = jnp.where(j < n_q_tiles, qg_ref[...], kg_ref[...])
        cs = cs_ref[...]
        sn = sn_ref[...]
        for hh in range(heads_per_tile):
            sl = slice(hh * HEAD_DIM, (hh + 1) * HEAD_DIM)
            y = _rms_rows(z[:, sl], gain)
            y = y * cs + pltpu.roll(y, HEAD_DIM // 2, axis=1) * sn
            o_ref[:, sl] = y.astype(BF16)

    @pl.when(j >= n_q_tiles + n_k_tiles)
    def _():
        o_ref[...] = z.astype(BF16)


def _glu_kernel(x_ref, g_ref, wu_ref, wg_ref, bu_ref, bg_ref, o_ref, h_ref):
    @pl.when(pl.program_id(1) == 0)
    def _():
        h_ref[...] = _rms_rows(x_ref[...], g_ref[...]).astype(BF16)

    h = h_ref[...]
    u = jnp.dot(h, wu_ref[...], preferred_element_type=F32) + bu_ref[...]
    ug = jnp.dot(h, wg_ref[...], preferred_element_type=F32) + bg_ref[...]
    o_ref[...] = (u * _sigmoid(ug)).astype(BF16)


def _gates_kernel(x_ref, g_ref, w_ref, b_ref, o_ref, h_ref):
    @pl.when(pl.program_id(1) == 0)
    def _():
        h_ref[...] = _rms_rows(x_ref[...], g_ref[...]).astype(BF16)

    z = jnp.dot(h_ref[...], w_ref[...], preferred_element_type=F32) + b_ref[...]
    o_ref[...] = _sigmoid(z).astype(BF16)


def _attn_kernel(q_ref, k_ref, v_ref, o_ref):
    k = k_ref[...]
    v = v_ref[...]
    for g in range(GQA_GROUP):
        sl = slice(g * HEAD_DIM, (g + 1) * HEAD_DIM)
        s = lax.dot_general(q_ref[:, sl], k, (((1,), (1,)), ((), ())),
                            preferred_element_type=F32)
        p = jnp.exp(s - jnp.max(s, axis=-1, keepdims=True))
        l = jnp.sum(p, axis=-1, keepdims=True)
        o = jnp.dot(p.astype(BF16), v, preferred_element_type=F32)
        o_ref[:, sl] = (o / l).astype(BF16)


def _conv_kernel(prev_ref, cur_ref, next_ref, w_ref, b_ref, lg_ref, lb_ref, o_ref, buf_ref,
                 conv_ref):
    si = pl.program_id(1)
    ts = cur_ref.shape[0]
    n_taps = w_ref.shape[0]
    pad = n_taps // 2
    prev = prev_ref[...].astype(F32)
    nxt = next_ref[...].astype(F32)
    buf_ref[0:HALO, :] = jnp.where(si > 0, prev, jnp.zeros_like(prev))
    buf_ref[HALO:HALO + ts, :] = cur_ref[...].astype(F32)
    buf_ref[HALO + ts:, :] = jnp.where(si < pl.num_programs(1) - 1, nxt, jnp.zeros_like(nxt))

    a_lo, a_hi = -((pad + SUBLANES - 1) // SUBLANES), pad // SUBLANES
    halo_groups = HALO // SUBLANES
    assert -a_lo <= halo_groups and a_hi + 1 <= halo_groups
    sub = lax.broadcasted_iota(jnp.int32, (SUBLANES, LANES), 0)

    for lane_group in range(buf_ref.shape[1] // LANES):
        ls = slice(lane_group * LANES, (lane_group + 1) * LANES)
        wv = [jnp.broadcast_to(w_ref[t:t + 1, ls], (SUBLANES, LANES)) for t in range(n_taps)]
        bias = jnp.broadcast_to(b_ref[:, ls], (SUBLANES, LANES))

        def load_group(gi, ls=ls):
            return buf_ref[pl.ds(pl.multiple_of(gi * SUBLANES, SUBLANES), SUBLANES), ls]

        def partial_sums(xs, wv=wv):
            ys = []
            for b in range(SUBLANES):
                y = None
                for ai, a in enumerate(range(a_lo, a_hi + 1)):
                    t = pad + SUBLANES * a + b
                    if 0 <= t < n_taps:
                        term = xs[ai] * wv[t]
                        y = term if y is None else y + term
                ys.append(y)
            return ys

        def body(g, carry, ls=ls, bias=bias, load_group=load_group, partial_sums=partial_sums):
            y_prev = carry[:SUBLANES]
            xs = list(carry[SUBLANES:]) + [load_group(g + 1 + a_hi + halo_groups)]
            y_new = partial_sums(xs)
            r = None
            for b in range(SUBLANES - 1, -1, -1):
                cb = y_prev[0] if b == 0 else jnp.where(sub >= b, y_prev[b], y_new[b])
                r = cb if r is None else cb + pltpu.roll(r, SUBLANES - 1, axis=0)
            conv_ref[pl.ds(pl.multiple_of(g * SUBLANES, SUBLANES), SUBLANES), ls] = r + bias
            return tuple(y_new) + tuple(xs[1:])

        x0 = [load_group(halo_groups + a_lo + i) for i in range(a_hi - a_lo + 1)]
        lax.fori_loop(0, ts // SUBLANES, body, tuple(partial_sums(x0)) + tuple(x0[1:]),
                      unroll=CONV_UNROLL)

    def layer_norm(r, carry):
        r0 = pl.multiple_of(r * LN_ROWS, LN_ROWS)
        acc = conv_ref[pl.ds(r0, LN_ROWS), :]
        mu = jnp.mean(acc, axis=-1, keepdims=True)
        d = acc - mu
        var = jnp.mean(d * d, axis=-1, keepdims=True)
        y = d * lax.rsqrt(var + EPS) * lg_ref[...] + lb_ref[...]
        o_ref[pl.ds(r0, LN_ROWS), :] = (y * _sigmoid(y)).astype(BF16)
        return carry

    lax.fori_loop(0, ts // LN_ROWS, layer_norm, 0)


def _merge_kernel(a_ref, c_ref, ga_ref, gc_ref, wa_ref, wc_ref, o_ref):
    pa = jnp.dot(a_ref[...], wa_ref[...], preferred_element_type=F32)
    pc = jnp.dot(c_ref[...], wc_ref[...], preferred_element_type=F32)
    o_ref[...] = (ga_ref[...].astype(F32) * pa + gc_ref[...].astype(F32) * pc).astype(BF16)


def _oproj_kernel(m_ref, w_ref, x_ref, o_ref):
    o_ref[...] = x_ref[...] + jnp.dot(m_ref[...], w_ref[...], preferred_element_type=F32)


def _ffn_kernel(n_seq_tiles, prev_ref, cur_ref, next_ref, g_ref, wg_ref, wv_ref, cw_ref, cb_ref,
                wd_ref, fg_ref, o_ref, h_ref, gt_ref, acc_ref):
    si = pl.program_id(0) % n_seq_tiles
    f = pl.program_id(1)
    tm = cur_ref.shape[0]

    @pl.when(f == 0)
    def _():
        g = g_ref[...]
        hp = _rms_rows(prev_ref[...], g)
        hn = _rms_rows(next_ref[...], g)
        h_ref[0:HALO, :] = jnp.where(si > 0, hp, jnp.zeros_like(hp)).astype(BF16)
        h_ref[HALO:HALO + tm, :] = _rms_rows(cur_ref[...], g).astype(BF16)
        h_ref[HALO + tm:, :] = jnp.where(si < n_seq_tiles - 1, hn, jnp.zeros_like(hn)).astype(BF16)
        acc_ref[...] = jnp.zeros_like(acc_ref)

    gt_ref[...] = jnp.dot(h_ref[...], wg_ref[...], preferred_element_type=F32)
    val = jnp.dot(h_ref[HALO:HALO + tm, :], wv_ref[...], preferred_element_type=F32)
    gt = (gt_ref[HALO - 1:HALO - 1 + tm, :] * cw_ref[0:1, :]
          + gt_ref[HALO:HALO + tm, :] * cw_ref[1:2, :]
          + gt_ref[HALO + 1:HALO + 1 + tm, :] * cw_ref[2:3, :]
          + cb_ref[...])
    act = (gt * _sigmoid(gt) * val).astype(BF16)
    acc_ref[...] += jnp.dot(act, wd_ref[...], preferred_element_type=F32)

    @pl.when(f == pl.num_programs(1) - 1)
    def _():
        o_ref[...] = _rms_rows(cur_ref[...] + acc_ref[...], fg_ref[...])


def _rope_tables(seq_len):
    rows = seq_len // GRID_W
    half = HEAD_DIM // 2
    row = jnp.repeat(jnp.arange(rows), GRID_W).astype(F32)
    col = jnp.tile(jnp.arange(GRID_W), rows).astype(F32)
    inv_freq = ROPE_THETA ** (-jnp.arange(0, half, 2, dtype=F32) / half)
    ang = jnp.concatenate([row[:, None] * inv_freq, col[:, None] * inv_freq], axis=-1)
    cos, sin = jnp.cos(ang), jnp.sin(ang)
    return jnp.concatenate([cos, cos], axis=-1), jnp.concatenate([-sin, sin], axis=-1)


def _row(v):
    return v.reshape(1, -1).astype(F32)


def kernel(x, norm_mix_g, w_in, b_in, q_norm_g, k_norm_g, w_attn_o, conv_dw_w, conv_dw_b,
           conv_ln_g, conv_ln_b, w_conv_o, w_out, norm_ffn_g, w_ffn_up, ffn_dw_w, ffn_dw_b,
           w_ffn_down, norm_final_g):
    B, S, D = x.shape
    T = B * S
    depth = w_in.shape[0]
    attn_q = N_Q_HEADS * HEAD_DIM
    attn_kv = N_KV_HEADS * HEAD_DIM
    qkv_cols = attn_q + 2 * attn_kv
    ffn_dim = w_ffn_down.shape[1]
    assert D % TN_PROJ == 0 and T % TM_PROJ == 0 and S % TM_PROJ == 0
    assert S % TQ_ATTN == 0 and S % TS_CONV == 0 and S % TM_FFN == 0 and ffn_dim % TF_FFN == 0

    deint = np.concatenate([np.arange(0, HEAD_DIM, 2), np.arange(1, HEAD_DIM, 2)])
    qk_perm = (np.arange(N_Q_HEADS + N_KV_HEADS)[:, None] * HEAD_DIM + deint[None, :]).reshape(-1)
    col_perm = np.concatenate([qk_perm, np.arange(qk_perm.size, w_in.shape[2])])
    cos_t, sin_t = _rope_tables(S)
    scale = HEAD_DIM ** -0.5

    xt = x.reshape(T, D)
    n_i = T // TM_PROJ
    for l in range(depth):
        w_in_l = w_in[l][:, col_perm].astype(BF16)
        b_in_l = _row(b_in[l][col_perm])
        g_mix = _row(norm_mix_g[l])
        qg = _row(q_norm_g[l][deint]) * scale
        kg = _row(k_norm_g[l][deint])

        x_spec = pl.BlockSpec((TM_PROJ, D), lambda i, j: (i, 0))
        g_spec = pl.BlockSpec((1, D), lambda i, j: (0, 0))
        h_scratch = pltpu.VMEM((TM_PROJ, D), BF16)
        seq_tiles = S // TM_PROJ

        qkv = pl.pallas_call(
            _qkv_kernel,
            grid=(n_i, qkv_cols // TN_PROJ),
            in_specs=[
                x_spec, g_spec,
                pl.BlockSpec((D, TN_PROJ), lambda i, j: (0, j)),
                pl.BlockSpec((1, TN_PROJ), lambda i, j: (0, j)),
                pl.BlockSpec((1, HEAD_DIM), lambda i, j: (0, 0)),
                pl.BlockSpec((1, HEAD_DIM), lambda i, j: (0, 0)),
                pl.BlockSpec((TM_PROJ, HEAD_DIM), lambda i, j: (i % seq_tiles, 0)),
                pl.BlockSpec((TM_PROJ, HEAD_DIM), lambda i, j: (i % seq_tiles, 0)),
            ],
            out_specs=pl.BlockSpec((TM_PROJ, TN_PROJ), lambda i, j: (i, j)),
            out_shape=jax.ShapeDtypeStruct((T, qkv_cols), BF16),
            scratch_shapes=[h_scratch],
            compiler_params=_params("parallel", "arbitrary"),
            name="qkv",
        )(xt, g_mix, w_in_l, b_in_l, qg, kg, cos_t, sin_t)

        tn_glu = TN_PROJ // 2
        u_off = qkv_cols // tn_glu
        ug_off = (qkv_cols + CONV_CH) // tn_glu
        glu = pl.pallas_call(
            _glu_kernel,
            grid=(n_i, CONV_CH // tn_glu),
            in_specs=[
                x_spec, g_spec,
                pl.BlockSpec((D, tn_glu), lambda i, j: (0, u_off + j)),
                pl.BlockSpec((D, tn_glu), lambda i, j: (0, ug_off + j)),
                pl.BlockSpec((1, tn_glu), lambda i, j: (0, u_off + j)),
                pl.BlockSpec((1, tn_glu), lambda i, j: (0, ug_off + j)),
            ],
            out_specs=pl.BlockSpec((TM_PROJ, tn_glu), lambda i, j: (i, j)),
            out_shape=jax.ShapeDtypeStruct((T, CONV_CH), BF16),
            scratch_shapes=[h_scratch],
            compiler_params=_params("parallel", "arbitrary"),
            name="glu",
        )(xt, g_mix, w_in_l, w_in_l, b_in_l, b_in_l)

        gate_off = (qkv_cols + 2 * CONV_CH) // TN_PROJ
        gates = pl.pallas_call(
            _gates_kernel,
            grid=(n_i, 2 * D // TN_PROJ),
            in_specs=[
                x_spec, g_spec,
                pl.BlockSpec((D, TN_PROJ), lambda i, j: (0, gate_off + j)),
                pl.BlockSpec((1, TN_PROJ), lambda i, j: (0, gate_off + j)),
            ],
            out_specs=pl.BlockSpec((TM_PROJ, TN_PROJ), lambda i, j: (i, j)),
            out_shape=jax.ShapeDtypeStruct((T, 2 * D), BF16),
            scratch_shapes=[h_scratch],
            compiler_params=_params("parallel", "arbitrary"),
            name="gates",
        )(xt, g_mix, w_in_l, b_in_l)

        gw = GQA_GROUP * HEAD_DIM
        n_qt = S // TQ_ATTN
        k_col0 = attn_q // HEAD_DIM
        v_col0 = (attn_q + attn_kv) // HEAD_DIM
        attn = pl.pallas_call(
            _attn_kernel,
            grid=(B, N_KV_HEADS, n_qt),
            in_specs=[
                pl.BlockSpec((TQ_ATTN, gw), lambda b, h, t: (b * n_qt + t, h)),
                pl.BlockSpec((S, HEAD_DIM), lambda b, h, t: (b, k_col0 + h)),
                pl.BlockSpec((S, HEAD_DIM), lambda b, h, t: (b, v_col0 + h)),
            ],
            out_specs=pl.BlockSpec((TQ_ATTN, gw), lambda b, h, t: (b * n_qt + t, h)),
            out_shape=jax.ShapeDtypeStruct((T, attn_q), BF16),
            compiler_params=_params("parallel", "parallel", "arbitrary"),
            name="attn",
        )(qkv, qkv, qkv)

        n_st = S // TS_CONV
        hb = TS_CONV // HALO
        n_hb = T // HALO
        conv = pl.pallas_call(
            _conv_kernel,
            grid=(B, n_st),
            in_specs=[
                pl.BlockSpec((HALO, CONV_CH), lambda b, s: (jnp.maximum((b * n_st + s) * hb - 1, 0), 0)),
                pl.BlockSpec((TS_CONV, CONV_CH), lambda b, s: (b * n_st + s, 0)),
                pl.BlockSpec((HALO, CONV_CH), lambda b, s: (jnp.minimum((b * n_st + s + 1) * hb, n_hb - 1), 0)),
                pl.BlockSpec(conv_dw_w.shape[1:], lambda b, s: (0, 0)),
                pl.BlockSpec((1, CONV_CH), lambda b, s: (0, 0)),
                pl.BlockSpec((1, CONV_CH), lambda b, s: (0, 0)),
                pl.BlockSpec((1, CONV_CH), lambda b, s: (0, 0)),
            ],
            out_specs=pl.BlockSpec((TS_CONV, CONV_CH), lambda b, s: (b * n_st + s, 0)),
            out_shape=jax.ShapeDtypeStruct((T, CONV_CH), BF16),
            scratch_shapes=[pltpu.VMEM((TS_CONV + 2 * HALO, CONV_CH), F32),
                            pltpu.VMEM((TS_CONV, CONV_CH), F32)],
            compiler_params=_params("parallel", "arbitrary"),
            name="conv",
        )(glu, glu, glu, conv_dw_w[l].astype(F32), _row(conv_dw_b[l]), _row(conv_ln_g[l]),
          _row(conv_ln_b[l]))

        n_dj = D // TN_PROJ
        merged = pl.pallas_call(
            _merge_kernel,
            grid=(n_i, n_dj),
            in_specs=[
                pl.BlockSpec((TM_PROJ, attn_q), lambda i, j: (i, 0)),
                pl.BlockSpec((TM_PROJ, CONV_CH), lambda i, j: (i, 0)),
                pl.BlockSpec((TM_PROJ, TN_PROJ), lambda i, j: (i, j)),
                pl.BlockSpec((TM_PROJ, TN_PROJ), lambda i, j: (i, n_dj + j)),
                pl.BlockSpec((attn_q, TN_PROJ), lambda i, j: (0, j)),
                pl.BlockSpec((CONV_CH, TN_PROJ), lambda i, j: (0, j)),
            ],
            out_specs=pl.BlockSpec((TM_PROJ, TN_PROJ), lambda i, j: (i, j)),
            out_shape=jax.ShapeDtypeStruct((T, D), BF16),
            compiler_params=_params("parallel", "arbitrary"),
            name="merge",
        )(attn, conv, gates, gates, w_attn_o[l].astype(BF16), w_conv_o[l].astype(BF16))

        x1 = pl.pallas_call(
            _oproj_kernel,
            grid=(n_i, n_dj),
            in_specs=[
                pl.BlockSpec((TM_PROJ, D), lambda i, j: (i, 0)),
                pl.BlockSpec((D, TN_PROJ), lambda i, j: (0, j)),
                pl.BlockSpec((TM_PROJ, TN_PROJ), lambda i, j: (i, j)),
            ],
            out_specs=pl.BlockSpec((TM_PROJ, TN_PROJ), lambda i, j: (i, j)),
            out_shape=jax.ShapeDtypeStruct((T, D), F32),
            compiler_params=_params("parallel", "arbitrary"),
            name="oproj",
        )(merged, w_out[l].astype(BF16), xt)

        n_ft = T // TM_FFN
        hbf = TM_FFN // HALO
        n_f = ffn_dim // TF_FFN
        w_up = w_ffn_up[l].astype(BF16)
        final_g = _row(norm_final_g) if l == depth - 1 else None
        assert final_g is not None, "intermediate layers need an un-normalised FFN epilogue"
        xt = pl.pallas_call(
            functools.partial(_ffn_kernel, S // TM_FFN),
            grid=(n_ft, n_f),
            in_specs=[
                pl.BlockSpec((HALO, D), lambda i, f: (jnp.maximum(i * hbf - 1, 0), 0)),
                pl.BlockSpec((TM_FFN, D), lambda i, f: (i, 0)),
                pl.BlockSpec((HALO, D), lambda i, f: (jnp.minimum((i + 1) * hbf, n_hb - 1), 0)),
                pl.BlockSpec((1, D), lambda i, f: (0, 0)),
                pl.BlockSpec((D, TF_FFN), lambda i, f: (0, f)),
                pl.BlockSpec((D, TF_FFN), lambda i, f: (0, n_f + f)),
                pl.BlockSpec((ffn_dw_w.shape[1], TF_FFN), lambda i, f: (0, f)),
                pl.BlockSpec((1, TF_FFN), lambda i, f: (0, f)),
                pl.BlockSpec((TF_FFN, D), lambda i, f: (f, 0)),
                pl.BlockSpec((1, D), lambda i, f: (0, 0)),
            ],
            out_specs=pl.BlockSpec((TM_FFN, D), lambda i, f: (i, 0)),
            out_shape=jax.ShapeDtypeStruct((T, D), F32),
            scratch_shapes=[
                pltpu.VMEM((TM_FFN + 2 * HALO, D), BF16),
                pltpu.VMEM((TM_FFN + 2 * HALO, TF_FFN), F32),
                pltpu.VMEM((TM_FFN, D), F32),
            ],
            compiler_params=_params("parallel", "arbitrary"),
            name="ffn",
        )(x1, x1, x1, _row(norm_ffn_g[l]), w_up, w_up, ffn_dw_w[l].astype(F32), _row(ffn_dw_b[l]),
          w_ffn_down[l].astype(BF16), final_g)
    return xt.reshape(B, S, D)
```

```python
import functools

import jax
import jax.numpy as jnp
from jax import lax
from jax.experimental import pallas as pl
from jax.experimental.pallas import tpu as pltpu

F32 = jnp.float32
BF16 = jnp.bfloat16

EPS = 1e-6
HEAD_DIM = 128
N_Q_HEADS = 8
N_KV_HEADS = 4
GQA_GROUP = N_Q_HEADS // N_KV_HEADS
ROPE_THETA = 10000.0
GRID_W = 64
CONV_CH = 1024

V7X_VMEM_BYTES = 64 * 1024 * 1024
VMEM_LIMIT = V7X_VMEM_BYTES - 8 * 1024 * 1024
SUBLANES = 8
LANES = 128
BF16_SUBLANES = 16

TM_PROJ = 1024
TN_IN = 1024
TN_PROJ = 512
PRO_CHUNKS = 4
TQ_ATTN = 512
GLU_ROWS = 128
CONV_UNROLL = 8
LN_ROWS = 128
TM_FFN = 512
TF_FFN = 512
HALO = BF16_SUBLANES


def _params(*sem):
    return pltpu.CompilerParams(dimension_semantics=sem, vmem_limit_bytes=VMEM_LIMIT)


def _rms_rows(x, g):
    ms = jnp.mean(x * x, axis=-1, keepdims=True)
    return x * lax.rsqrt(ms + EPS) * g


def _sigmoid(x):
    return 0.5 * jnp.tanh(0.5 * x) + 0.5


def _inproj_kernel(x_ref, g_ref, w_ref, b_ref, qg_ref, kg_ref, rc_ref, ra_ref, rb_ref, o_ref,
                   ha_ref, hb_ref):
    i = pl.program_id(0)
    j = pl.program_id(1)
    tm, tn = o_ref.shape
    q_steps = N_Q_HEADS * HEAD_DIM // tn
    k_heads_in_step = N_KV_HEADS
    assert q_steps * tn == N_Q_HEADS * HEAD_DIM and k_heads_in_step * HEAD_DIM <= tn

    @pl.when((i == 0) & (j == 0))
    def _():
        ha_ref[...] = _rms_rows(x_ref[...], g_ref[...]).astype(BF16)

    rows = tm // PRO_CHUNKS
    r0 = pl.multiple_of(jnp.clip(j - 1, 0, PRO_CHUNKS - 1) * rows, rows)

    def rope_heads(z, gain, n_heads):
        rc, ra, rb = rc_ref[...], ra_ref[...], rb_ref[...]
        for hh in range(n_heads):
            sl = slice(hh * HEAD_DIM, (hh + 1) * HEAD_DIM)
            y = _rms_rows(z[:, sl], gain)
            y = (y * rc + pltpu.roll(y, HEAD_DIM - 1, axis=1) * ra
                 + pltpu.roll(y, 1, axis=1) * rb)
            o_ref[:, sl] = y.astype(BF16)

    def step(h_cur, h_next):
        h_next[pl.ds(r0, rows), :] = _rms_rows(x_ref[pl.ds(r0, rows), :], g_ref[...]).astype(BF16)
        z = jnp.dot(h_cur[...], w_ref[...], preferred_element_type=F32) + b_ref[...]
        o_ref[...] = z.astype(BF16)

        @pl.when(j < q_steps)
        def _():
            rope_heads(z, qg_ref[...], tn // HEAD_DIM)

        @pl.when(j == q_steps)
        def _():
            rope_heads(z, kg_ref[...], k_heads_in_step)

    @pl.when(i % 2 == 0)
    def _():
        step(ha_ref, hb_ref)

    @pl.when(i % 2 == 1)
    def _():
        step(hb_ref, ha_ref)


def _attn_kernel(q_ref, k_ref, v_ref, o_ref):
    k = k_ref[...]
    v = v_ref[...]
    for g in range(GQA_GROUP):
        sl = slice(g * HEAD_DIM, (g + 1) * HEAD_DIM)
        for qb in range(q_ref.shape[0] // TQ_ATTN):
            rows = slice(qb * TQ_ATTN, (qb + 1) * TQ_ATTN)
            s = lax.dot_general(q_ref[rows, sl], k, (((1,), (1,)), ((), ())),
                                preferred_element_type=F32)
            p = jnp.exp(s - jnp.max(s, axis=-1, keepdims=True))
            l = jnp.sum(p, axis=-1, keepdims=True)
            o = jnp.dot(p.astype(BF16), v, preferred_element_type=F32)
            o_ref[rows, sl] = (o / l).astype(BF16)


def _conv_kernel(u_ref, ug_ref, w_ref, b_ref, lg_ref, lb_ref, o_ref, buf_ref, conv_ref):
    ts = u_ref.shape[0]
    n_taps = w_ref.shape[0]
    pad = n_taps // 2
    buf_ref[0:HALO, :] = jnp.zeros((HALO, buf_ref.shape[1]), F32)
    buf_ref[HALO + ts:, :] = jnp.zeros((HALO, buf_ref.shape[1]), F32)

    def glu(r, carry):
        r0 = pl.multiple_of(r * GLU_ROWS, GLU_ROWS)
        u = u_ref[pl.ds(r0, GLU_ROWS), :].astype(F32)
        ug = ug_ref[pl.ds(r0, GLU_ROWS), :].astype(F32)
        buf_ref[pl.ds(HALO + r0, GLU_ROWS), :] = u * _sigmoid(ug)
        return carry

    lax.fori_loop(0, ts // GLU_ROWS, glu, 0)

    a_lo, a_hi = -((pad + SUBLANES - 1) // SUBLANES), pad // SUBLANES
    halo_groups = HALO // SUBLANES
    assert -a_lo <= halo_groups and a_hi + 1 <= halo_groups
    sub = lax.broadcasted_iota(jnp.int32, (SUBLANES, LANES), 0)

    for lane_group in range(buf_ref.shape[1] // LANES):
        ls = slice(lane_group * LANES, (lane_group + 1) * LANES)
        wv = [jnp.broadcast_to(w_ref[t:t + 1, ls], (SUBLANES, LANES)) for t in range(n_taps)]
        bias = jnp.broadcast_to(b_ref[:, ls], (SUBLANES, LANES))

        def load_group(gi, ls=ls):
            return buf_ref[pl.ds(pl.multiple_of(gi * SUBLANES, SUBLANES), SUBLANES), ls]

        def partial_sums(xs, wv=wv):
            ys = []
            for b in range(SUBLANES):
                y = None
                for ai, a in enumerate(range(a_lo, a_hi + 1)):
                    t = pad + SUBLANES * a + b
                    if 0 <= t < n_taps:
                        term = xs[ai] * wv[t]
                        y = term if y is None else y + term
                ys.append(y)
            return ys

        def body(g, carry, ls=ls, bias=bias, load_group=load_group, partial_sums=partial_sums):
            y_prev = carry[:SUBLANES]
            xs = list(carry[SUBLANES:]) + [load_group(g + 1 + a_hi + halo_groups)]
            y_new = partial_sums(xs)
            r = None
            for b in range(SUBLANES - 1, -1, -1):
                cb = y_prev[0] if b == 0 else jnp.where(sub >= b, y_prev[b], y_new[b])
                r = cb if r is None else cb + pltpu.roll(r, SUBLANES - 1, axis=0)
            conv_ref[pl.ds(pl.multiple_of(g * SUBLANES, SUBLANES), SUBLANES), ls] = r + bias
            return tuple(y_new) + tuple(xs[1:])

        x0 = [load_group(halo_groups + a_lo + i) for i in range(a_hi - a_lo + 1)]
        lax.fori_loop(0, ts // SUBLANES, body, tuple(partial_sums(x0)) + tuple(x0[1:]),
                      unroll=CONV_UNROLL)

    def layer_norm(r, carry):
        r0 = pl.multiple_of(r * LN_ROWS, LN_ROWS)
        acc = conv_ref[pl.ds(r0, LN_ROWS), :]
        mu = jnp.mean(acc, axis=-1, keepdims=True)
        d = acc - mu
        var = jnp.mean(d * d, axis=-1, keepdims=True)
        y = d * lax.rsqrt(var + EPS) * lg_ref[...] + lb_ref[...]
        o_ref[pl.ds(r0, LN_ROWS), :] = (y * _sigmoid(y)).astype(BF16)
        return carry

    lax.fori_loop(0, ts // LN_ROWS, layer_norm, 0)


def _merge_kernel(a_ref, c_ref, ga_ref, gc_ref, wa_ref, wc_ref, o_ref):
    pa = jnp.dot(a_ref[...], wa_ref[...], preferred_element_type=F32)
    pc = jnp.dot(c_ref[...], wc_ref[...], preferred_element_type=F32)
    ga = _sigmoid(ga_ref[...].astype(F32))
    gc = _sigmoid(gc_ref[...].astype(F32))
    o_ref[...] = (ga * pa + gc * pc).astype(BF16)


def _oproj_kernel(m_ref, w_ref, x_ref, o_ref):
    o_ref[...] = x_ref[...] + jnp.dot(m_ref[...], w_ref[...], preferred_element_type=F32)


def _ffn_kernel(n_seq_tiles, prev_ref, cur_ref, next_ref, g_ref, wg_ref, wv_ref, cw_ref, cb_ref,
                wd_ref, fg_ref, o_ref, h_ref, gt_ref, acc_ref):
    si = pl.program_id(0) % n_seq_tiles
    f = pl.program_id(1)
    tm = cur_ref.shape[0]

    @pl.when(f == 0)
    def _():
        g = g_ref[...]
        hp = _rms_rows(prev_ref[...], g)
        hn = _rms_rows(next_ref[...], g)
        h_ref[0:HALO, :] = jnp.where(si > 0, hp, jnp.zeros_like(hp)).astype(BF16)
        h_ref[HALO:HALO + tm, :] = _rms_rows(cur_ref[...], g).astype(BF16)
        h_ref[HALO + tm:, :] = jnp.where(si < n_seq_tiles - 1, hn, jnp.zeros_like(hn)).astype(BF16)
        acc_ref[...] = jnp.zeros_like(acc_ref)

    gt_ref[...] = jnp.dot(h_ref[...], wg_ref[...], preferred_element_type=F32)
    val = jnp.dot(h_ref[HALO:HALO + tm, :], wv_ref[...], preferred_element_type=F32)
    gt = (gt_ref[HALO - 1:HALO - 1 + tm, :] * cw_ref[0:1, :]
          + gt_ref[HALO:HALO + tm, :] * cw_ref[1:2, :]
          + gt_ref[HALO + 1:HALO + 1 + tm, :] * cw_ref[2:3, :]
          + cb_ref[...])
    act = (gt * _sigmoid(gt) * val).astype(BF16)
    acc_ref[...] += jnp.dot(act, wd_ref[...], preferred_element_type=F32)

    @pl.when(f == pl.num_programs(1) - 1)
    def _():
        o_ref[...] = _rms_rows(cur_ref[...] + acc_ref[...], fg_ref[...])


def _rope_tables(seq_len):
    rows = seq_len // GRID_W
    half = HEAD_DIM // 2
    row = jnp.repeat(jnp.arange(rows), GRID_W).astype(F32)
    col = jnp.tile(jnp.arange(GRID_W), rows).astype(F32)
    inv_freq = ROPE_THETA ** (-jnp.arange(0, half, 2, dtype=F32) / half)
    ang = jnp.concatenate([row[:, None] * inv_freq, col[:, None] * inv_freq], axis=-1)
    cos = jnp.repeat(jnp.cos(ang), 2, axis=-1)
    sin = jnp.repeat(jnp.sin(ang), 2, axis=-1)
    even = (jnp.arange(HEAD_DIM) % 2 == 0)[None, :]
    return cos, jnp.where(even, -sin, 0.0), jnp.where(even, 0.0, sin)


def _row(v):
    return v.reshape(1, -1).astype(F32)


def kernel(x, norm_mix_g, w_in, b_in, q_norm_g, k_norm_g, w_attn_o, conv_dw_w, conv_dw_b,
           conv_ln_g, conv_ln_b, w_conv_o, w_out, norm_ffn_g, w_ffn_up, ffn_dw_w, ffn_dw_b,
           w_ffn_down, norm_final_g):
    B, S, D = x.shape
    T = B * S
    assert w_in.shape[0] == 1, "single trunk layer: the final rmsnorm is fused into the FFN"
    attn_q = N_Q_HEADS * HEAD_DIM
    attn_kv = N_KV_HEADS * HEAD_DIM
    in_cols = w_in.shape[2]
    ffn_dim = w_ffn_down.shape[1]
    u_col = attn_q + 2 * attn_kv
    gate_col = u_col + 2 * CONV_CH
    assert in_cols == gate_col + 2 * D
    assert T % TM_PROJ == 0 and S % TM_PROJ == 0 and in_cols % TN_IN == 0
    assert attn_q % TN_IN == 0 and u_col % CONV_CH == 0 and D % TN_PROJ == 0
    assert S % TQ_ATTN == 0 and S % TM_FFN == 0 and ffn_dim % TF_FFN == 0

    rope_c, rope_a, rope_b = _rope_tables(S)
    xt = x.reshape(T, D)
    n_i = T // TM_PROJ
    seq_tiles = S // TM_PROJ

    const2 = lambda i, j: (0, 0)
    rope_spec = pl.BlockSpec((TM_PROJ, HEAD_DIM), lambda i, j: (i % seq_tiles, 0))
    z = pl.pallas_call(
        _inproj_kernel,
        grid=(n_i, in_cols // TN_IN),
        in_specs=[
            pl.BlockSpec((TM_PROJ, D), lambda i, j: (
                jnp.where((i == 0) & (j == 0), 0, jnp.minimum(i + 1, n_i - 1)), 0)),
            pl.BlockSpec((1, D), const2),
            pl.BlockSpec((D, TN_IN), lambda i, j: (0, j)),
            pl.BlockSpec((1, TN_IN), lambda i, j: (0, j)),
            pl.BlockSpec((1, HEAD_DIM), const2),
            pl.BlockSpec((1, HEAD_DIM), const2),
            rope_spec, rope_spec, rope_spec,
        ],
        out_specs=pl.BlockSpec((TM_PROJ, TN_IN), lambda i, j: (i, j)),
        out_shape=jax.ShapeDtypeStruct((T, in_cols), BF16),
        scratch_shapes=[pltpu.VMEM((TM_PROJ, D), BF16), pltpu.VMEM((TM_PROJ, D), BF16)],
        compiler_params=_params("arbitrary", "arbitrary"),
        name="inproj",
    )(xt, _row(norm_mix_g[0]), w_in[0].astype(BF16), _row(b_in[0]),
      _row(q_norm_g[0]) * HEAD_DIM ** -0.5, _row(k_norm_g[0]), rope_c, rope_a, rope_b)

    gw = GQA_GROUP * HEAD_DIM
    k_col0 = attn_q // HEAD_DIM
    v_col0 = (attn_q + attn_kv) // HEAD_DIM
    attn = pl.pallas_call(
        _attn_kernel,
        grid=(B, N_KV_HEADS),
        in_specs=[
            pl.BlockSpec((S, gw), lambda b, h: (b, h)),
            pl.BlockSpec((S, HEAD_DIM), lambda b, h: (b, k_col0 + h)),
            pl.BlockSpec((S, HEAD_DIM), lambda b, h: (b, v_col0 + h)),
        ],
        out_specs=pl.BlockSpec((S, gw), lambda b, h: (b, h)),
        out_shape=jax.ShapeDtypeStruct((T, attn_q), BF16),
        compiler_params=_params("parallel", "parallel"),
        name="attn",
    )(z, z, z)

    conv = pl.pallas_call(
        _conv_kernel,
        grid=(B,),
        in_specs=[
            pl.BlockSpec((S, CONV_CH), lambda b: (b, u_col // CONV_CH)),
            pl.BlockSpec((S, CONV_CH), lambda b: (b, u_col // CONV_CH + 1)),
            pl.BlockSpec(conv_dw_w.shape[1:], lambda b: (0, 0)),
            pl.BlockSpec((1, CONV_CH), lambda b: (0, 0)),
            pl.BlockSpec((1, CONV_CH), lambda b: (0, 0)),
            pl.BlockSpec((1, CONV_CH), lambda b: (0, 0)),
        ],
        out_specs=pl.BlockSpec((S, CONV_CH), lambda b: (b, 0)),
        out_shape=jax.ShapeDtypeStruct((T, CONV_CH), BF16),
        scratch_shapes=[pltpu.VMEM((S + 2 * HALO, CONV_CH), F32),
                        pltpu.VMEM((S, CONV_CH), F32)],
        compiler_params=_params("parallel"),
        name="conv",
    )(z, z, conv_dw_w[0].astype(F32), _row(conv_dw_b[0]), _row(conv_ln_g[0]), _row(conv_ln_b[0]))

    n_dj = D // TN_PROJ
    ga_blk = gate_col // TN_PROJ
    merged = pl.pallas_call(
        _merge_kernel,
        grid=(n_i, n_dj),
        in_specs=[
            pl.BlockSpec((TM_PROJ, attn_q), lambda i, j: (i, 0)),
            pl.BlockSpec((TM_PROJ, CONV_CH), lambda i, j: (i, 0)),
            pl.BlockSpec((TM_PROJ, TN_PROJ), lambda i, j: (i, ga_blk + j)),
            pl.BlockSpec((TM_PROJ, TN_PROJ), lambda i, j: (i, ga_blk + n_dj + j)),
            pl.BlockSpec((attn_q, TN_PROJ), lambda i, j: (0, j)),
            pl.BlockSpec((CONV_CH, TN_PROJ), lambda i, j: (0, j)),
        ],
        out_specs=pl.BlockSpec((TM_PROJ, TN_PROJ), lambda i, j: (i, j)),
        out_shape=jax.ShapeDtypeStruct((T, D), BF16),
        compiler_params=_params("parallel", "arbitrary"),
        name="merge",
    )(attn, conv, z, z, w_attn_o[0].astype(BF16), w_conv_o[0].astype(BF16))

    x1 = pl.pallas_call(
        _oproj_kernel,
        grid=(n_i, n_dj),
        in_specs=[
            pl.BlockSpec((TM_PROJ, D), lambda i, j: (i, 0)),
            pl.BlockSpec((D, TN_PROJ), lambda i, j: (0, j)),
            pl.BlockSpec((TM_PROJ, TN_PROJ), lambda i, j: (i, j)),
        ],
        out_specs=pl.BlockSpec((TM_PROJ, TN_PROJ), lambda i, j: (i, j)),
        out_shape=jax.ShapeDtypeStruct((T, D), F32),
        compiler_params=_params("parallel", "arbitrary"),
        name="oproj",
    )(merged, w_out[0].astype(BF16), xt)

    n_ft = T // TM_FFN
    hbf = TM_FFN // HALO
    n_hb = T // HALO
    n_f = ffn_dim // TF_FFN
    w_up = w_ffn_up[0].astype(BF16)
    y = pl.pallas_call(
        functools.partial(_ffn_kernel, S // TM_FFN),
        grid=(n_ft, n_f),
        in_specs=[
            pl.BlockSpec((HALO, D), lambda i, f: (jnp.maximum(i * hbf - 1, 0), 0)),
            pl.BlockSpec((TM_FFN, D), lambda i, f: (i, 0)),
            pl.BlockSpec((HALO, D), lambda i, f: (jnp.minimum((i + 1) * hbf, n_hb - 1), 0)),
            pl.BlockSpec((1, D), lambda i, f: (0, 0)),
            pl.BlockSpec((D, TF_FFN), lambda i, f: (0, f)),
            pl.BlockSpec((D, TF_FFN), lambda i, f: (0, n_f + f)),
            pl.BlockSpec((ffn_dw_w.shape[1], TF_FFN), lambda i, f: (0, f)),
            pl.BlockSpec((1, TF_FFN), lambda i, f: (0, f)),
            pl.BlockSpec((TF_FFN, D), lambda i, f: (f, 0)),
            pl.BlockSpec((1, D), lambda i, f: (0, 0)),
        ],
        out_specs=pl.BlockSpec((TM_FFN, D), lambda i, f: (i, 0)),
        out_shape=jax.ShapeDtypeStruct((T, D), F32),
        scratch_shapes=[
            pltpu.VMEM((TM_FFN + 2 * HALO, D), BF16),
            pltpu.VMEM((TM_FFN + 2 * HALO, TF_FFN), F32),
            pltpu.VMEM((TM_FFN, D), F32),
        ],
        compiler_params=_params("parallel", "arbitrary"),
        name="ffn",
    )(x1, x1, x1, _row(norm_ffn_g[0]), w_up, w_up, ffn_dw_w[0].astype(F32), _row(ffn_dw_b[0]),
      w_ffn_down[0].astype(BF16), _row(norm_final_g))
    return y.reshape(B, S, D)
```

```python
import functools

import jax
import jax.numpy as jnp
from jax import lax
from jax.experimental import pallas as pl
from jax.experimental.pallas import tpu as pltpu

F32 = jnp.float32
BF16 = jnp.bfloat16

EPS = 1e-6
HEAD_DIM = 128
N_Q_HEADS = 8
N_KV_HEADS = 4
GQA_GROUP = N_Q_HEADS // N_KV_HEADS
ROPE_THETA = 10000.0
GRID_W = 64
CONV_CH = 1024

V7X_VMEM_BYTES = 64 * 1024 * 1024
VMEM_LIMIT = V7X_VMEM_BYTES - 8 * 1024 * 1024
SUBLANES = 8
LANES = 128
BF16_SUBLANES = 16

TM_PROJ = 1024
TN_IN = 1024
TM_MIX = 512
TN_PROJ = 512
PRO_CHUNKS = 4
TQ_ATTN = 512
GLU_ROWS = 128
CONV_UNROLL = 8
LN_ROWS = 128
TM_FFN = 512
TF_FFN = 512
HALO = BF16_SUBLANES


def _params(*sem):
    return pltpu.CompilerParams(dimension_semantics=sem, vmem_limit_bytes=VMEM_LIMIT)


def _rms_rows(x, g):
    ms = jnp.mean(x * x, axis=-1, keepdims=True)
    return x * lax.rsqrt(ms + EPS) * g


def _sigmoid(x):
    return 0.5 * jnp.tanh(0.5 * x) + 0.5


def _inproj_kernel(x_ref, g_ref, w_ref, b_ref, qg_ref, kg_ref, rc_ref, ra_ref, rb_ref, o_ref,
                   ha_ref, hb_ref):
    i = pl.program_id(0)
    j = pl.program_id(1)
    tm, tn = o_ref.shape
    q_steps = N_Q_HEADS * HEAD_DIM // tn
    k_heads_in_step = N_KV_HEADS
    assert q_steps * tn == N_Q_HEADS * HEAD_DIM and k_heads_in_step * HEAD_DIM <= tn

    @pl.when((i == 0) & (j == 0))
    def _():
        ha_ref[...] = _rms_rows(x_ref[...], g_ref[...]).astype(BF16)

    rows = tm // PRO_CHUNKS
    r0 = pl.multiple_of(jnp.clip(j - 1, 0, PRO_CHUNKS - 1) * rows, rows)

    def rope_heads(z, gain, n_heads):
        rc, ra, rb = rc_ref[...], ra_ref[...], rb_ref[...]
        for hh in range(n_heads):
            sl = slice(hh * HEAD_DIM, (hh + 1) * HEAD_DIM)
            y = _rms_rows(z[:, sl], gain)
            y = (y * rc + pltpu.roll(y, HEAD_DIM - 1, axis=1) * ra
                 + pltpu.roll(y, 1, axis=1) * rb)
            o_ref[:, sl] = y.astype(BF16)

    def step(h_cur, h_next):
        h_next[pl.ds(r0, rows), :] = _rms_rows(x_ref[pl.ds(r0, rows), :], g_ref[...]).astype(BF16)
        z = jnp.dot(h_cur[...], w_ref[...], preferred_element_type=F32) + b_ref[...]
        o_ref[...] = z.astype(BF16)

        @pl.when(j < q_steps)
        def _():
            rope_heads(z, qg_ref[...], tn // HEAD_DIM)

        @pl.when(j == q_steps)
        def _():
            rope_heads(z, kg_ref[...], k_heads_in_step)

    @pl.when(i % 2 == 0)
    def _():
        step(ha_ref, hb_ref)

    @pl.when(i % 2 == 1)
    def _():
        step(hb_ref, ha_ref)


def _attn_kernel(q_ref, k_ref, v_ref, o_ref):
    k = k_ref[...]
    v = v_ref[...]
    for g in range(GQA_GROUP):
        sl = slice(g * HEAD_DIM, (g + 1) * HEAD_DIM)
        for qb in range(q_ref.shape[0] // TQ_ATTN):
            rows = slice(qb * TQ_ATTN, (qb + 1) * TQ_ATTN)
            s = lax.dot_general(q_ref[rows, sl], k, (((1,), (1,)), ((), ())),
                                preferred_element_type=F32)
            p = jnp.exp(s - jnp.max(s, axis=-1, keepdims=True))
            l = jnp.sum(p, axis=-1, keepdims=True)
            o = jnp.dot(p.astype(BF16), v, preferred_element_type=F32)
            o_ref[rows, sl] = (o / l).astype(BF16)


def _conv_kernel(u_ref, ug_ref, w_ref, b_ref, lg_ref, lb_ref, o_ref, buf_ref, conv_ref):
    ts = u_ref.shape[0]
    n_taps = w_ref.shape[0]
    pad = n_taps // 2
    buf_ref[0:HALO, :] = jnp.zeros((HALO, buf_ref.shape[1]), F32)
    buf_ref[HALO + ts:, :] = jnp.zeros((HALO, buf_ref.shape[1]), F32)

    def glu(r, carry):
        r0 = pl.multiple_of(r * GLU_ROWS, GLU_ROWS)
        u = u_ref[pl.ds(r0, GLU_ROWS), :].astype(F32)
        ug = ug_ref[pl.ds(r0, GLU_ROWS), :].astype(F32)
        buf_ref[pl.ds(HALO + r0, GLU_ROWS), :] = u * _sigmoid(ug)
        return carry

    lax.fori_loop(0, ts // GLU_ROWS, glu, 0)

    a_lo, a_hi = -((pad + SUBLANES - 1) // SUBLANES), pad // SUBLANES
    halo_groups = HALO // SUBLANES
    assert -a_lo <= halo_groups and a_hi + 1 <= halo_groups
    sub = lax.broadcasted_iota(jnp.int32, (SUBLANES, LANES), 0)

    for lane_group in range(buf_ref.shape[1] // LANES):
        ls = slice(lane_group * LANES, (lane_group + 1) * LANES)
        wv = [jnp.broadcast_to(w_ref[t:t + 1, ls], (SUBLANES, LANES)) for t in range(n_taps)]
        bias = jnp.broadcast_to(b_ref[:, ls], (SUBLANES, LANES))

        def load_group(gi, ls=ls):
            return buf_ref[pl.ds(pl.multiple_of(gi * SUBLANES, SUBLANES), SUBLANES), ls]

        def partial_sums(xs, wv=wv):
            ys = []
            for b in range(SUBLANES):
                y = None
                for ai, a in enumerate(range(a_lo, a_hi + 1)):
                    t = pad + SUBLANES * a + b
                    if 0 <= t < n_taps:
                        term = xs[ai] * wv[t]
                        y = term if y is None else y + term
                ys.append(y)
            return ys

        def body(g, carry, ls=ls, bias=bias, load_group=load_group, partial_sums=partial_sums):
            y_prev = carry[:SUBLANES]
            xs = list(carry[SUBLANES:]) + [load_group(g + 1 + a_hi + halo_groups)]
            y_new = partial_sums(xs)
            r = None
            for b in range(SUBLANES - 1, -1, -1):
                cb = y_prev[0] if b == 0 else jnp.where(sub >= b, y_prev[b], y_new[b])
                r = cb if r is None else cb + pltpu.roll(r, SUBLANES - 1, axis=0)
            conv_ref[pl.ds(pl.multiple_of(g * SUBLANES, SUBLANES), SUBLANES), ls] = r + bias
            return tuple(y_new) + tuple(xs[1:])

        x0 = [load_group(halo_groups + a_lo + i) for i in range(a_hi - a_lo + 1)]
        lax.fori_loop(0, ts // SUBLANES, body, tuple(partial_sums(x0)) + tuple(x0[1:]),
                      unroll=CONV_UNROLL)

    def layer_norm(r, carry):
        r0 = pl.multiple_of(r * LN_ROWS, LN_ROWS)
        acc = conv_ref[pl.ds(r0, LN_ROWS), :]
        mu = jnp.mean(acc, axis=-1, keepdims=True)
        d = acc - mu
        var = jnp.mean(d * d, axis=-1, keepdims=True)
        y = d * lax.rsqrt(var + EPS) * lg_ref[...] + lb_ref[...]
        o_ref[pl.ds(r0, LN_ROWS), :] = (y * _sigmoid(y)).astype(BF16)
        return carry

    lax.fori_loop(0, ts // LN_ROWS, layer_norm, 0)


def _mixout_kernel(a_ref, c_ref, ga_ref, gc_ref, x_ref, wa_ref, wc_ref, wo_ref, fg_ref,
                   x1_ref, h2_ref, m_ref):
    a = a_ref[...]
    c = c_ref[...]
    n_chunks = x_ref.shape[1] // TN_PROJ
    for jc in range(n_chunks):
        cs = slice(jc * TN_PROJ, (jc + 1) * TN_PROJ)
        pa = jnp.dot(a, wa_ref[:, cs], preferred_element_type=F32)
        pc = jnp.dot(c, wc_ref[:, cs], preferred_element_type=F32)
        ga = _sigmoid(ga_ref[:, cs].astype(F32))
        gc = _sigmoid(gc_ref[:, cs].astype(F32))
        m_ref[:, cs] = (ga * pa + gc * pc).astype(BF16)
    m = m_ref[...]
    for jc in range(n_chunks):
        cs = slice(jc * TN_PROJ, (jc + 1) * TN_PROJ)
        x1_ref[:, cs] = x_ref[:, cs] + jnp.dot(m, wo_ref[:, cs], preferred_element_type=F32)
    h2_ref[...] = _rms_rows(x1_ref[...], fg_ref[...]).astype(BF16)


def _ffn_kernel(n_f, n_tiles, n_seq_tiles, prev_ref, cur_ref, next_ref, res_ref, wg_ref, wv_ref,
                cw_ref, cb_ref, wd_ref, fg_ref, o_ref, h_ref, gt_ref, acta_ref, actb_ref, acc_ref):
    s = pl.program_id(0)
    n_up = n_f * n_tiles
    f_up = s % n_f
    si = (s // n_f) % n_seq_tiles
    tm = cur_ref.shape[0]

    def load_tile():
        prev = prev_ref[...]
        nxt = next_ref[...]
        h_ref[0:HALO, :] = jnp.where(si > 0, prev, jnp.zeros_like(prev))
        h_ref[HALO:HALO + tm, :] = cur_ref[...]
        h_ref[HALO + tm:, :] = jnp.where(si < n_seq_tiles - 1, nxt, jnp.zeros_like(nxt))

    def up(act_ref):
        gt_ref[...] = jnp.dot(h_ref[...], wg_ref[...], preferred_element_type=F32)
        val = jnp.dot(h_ref[HALO:HALO + tm, :], wv_ref[...], preferred_element_type=F32)
        gt = (gt_ref[HALO - 1:HALO - 1 + tm, :] * cw_ref[0:1, :]
              + gt_ref[HALO:HALO + tm, :] * cw_ref[1:2, :]
              + gt_ref[HALO + 1:HALO + 1 + tm, :] * cw_ref[2:3, :]
              + cb_ref[...])
        act_ref[...] = (gt * _sigmoid(gt) * val).astype(BF16)

    def down(act_ref, first):
        d = jnp.dot(act_ref[...], wd_ref[...], preferred_element_type=F32)
        if first:
            acc_ref[...] = d
        else:
            acc_ref[...] += d

    def finish_tile():
        o_ref[...] = _rms_rows(res_ref[...] + acc_ref[...], fg_ref[...])

    acts = (acta_ref, actb_ref)
    for parity in (0, 1):
        a_new, a_old = acts[parity], acts[1 - parity]
        mine = (s % 2) == parity

        @pl.when(mine & (s == 0))
        def _():
            load_tile()
            up(a_new)

        @pl.when(mine & (f_up == 0) & (s > 0) & (s < n_up))
        def _():
            down(a_old, False)
            finish_tile()
            load_tile()
            up(a_new)

        @pl.when(mine & (f_up == 1))
        def _():
            up(a_new)
            down(a_old, True)

        @pl.when(mine & (f_up >= 2))
        def _():
            up(a_new)
            down(a_old, False)

        @pl.when(mine & (s == n_up))
        def _():
            down(a_old, False)
            finish_tile()


def _rope_tables(seq_len):
    rows = seq_len // GRID_W
    half = HEAD_DIM // 2
    row = jnp.repeat(jnp.arange(rows), GRID_W).astype(F32)
    col = jnp.tile(jnp.arange(GRID_W), rows).astype(F32)
    inv_freq = ROPE_THETA ** (-jnp.arange(0, half, 2, dtype=F32) / half)
    ang = jnp.concatenate([row[:, None] * inv_freq, col[:, None] * inv_freq], axis=-1)
    cos = jnp.repeat(jnp.cos(ang), 2, axis=-1)
    sin = jnp.repeat(jnp.sin(ang), 2, axis=-1)
    even = (jnp.arange(HEAD_DIM) % 2 == 0)[None, :]
    return cos, jnp.where(even, -sin, 0.0), jnp.where(even, 0.0, sin)


def _row(v):
    return v.reshape(1, -1).astype(F32)


def kernel(x, norm_mix_g, w_in, b_in, q_norm_g, k_norm_g, w_attn_o, conv_dw_w, conv_dw_b,
           conv_ln_g, conv_ln_b, w_conv_o, w_out, norm_ffn_g, w_ffn_up, ffn_dw_w, ffn_dw_b,
           w_ffn_down, norm_final_g):
    B, S, D = x.shape
    T = B * S
    assert w_in.shape[0] == 1, "single trunk layer: the final rmsnorm is fused into the FFN"
    attn_q = N_Q_HEADS * HEAD_DIM
    attn_kv = N_KV_HEADS * HEAD_DIM
    in_cols = w_in.shape[2]
    ffn_dim = w_ffn_down.shape[1]
    u_col = attn_q + 2 * attn_kv
    gate_col = u_col + 2 * CONV_CH
    assert in_cols == gate_col + 2 * D
    assert T % TM_PROJ == 0 and S % TM_PROJ == 0 and in_cols % TN_IN == 0
    assert attn_q % TN_IN == 0 and u_col % CONV_CH == 0 and D % TN_PROJ == 0
    assert S % TQ_ATTN == 0 and S % TM_FFN == 0 and ffn_dim % TF_FFN == 0 and T % TM_MIX == 0

    rope_c, rope_a, rope_b = _rope_tables(S)
    xt = x.reshape(T, D)
    n_i = T // TM_PROJ
    seq_tiles = S // TM_PROJ

    const2 = lambda i, j: (0, 0)
    rope_spec = pl.BlockSpec((TM_PROJ, HEAD_DIM), lambda i, j: (i % seq_tiles, 0))
    z = pl.pallas_call(
        _inproj_kernel,
        grid=(n_i, in_cols // TN_IN),
        in_specs=[
            pl.BlockSpec((TM_PROJ, D), lambda i, j: (
                jnp.where((i == 0) & (j == 0), 0, jnp.minimum(i + 1, n_i - 1)), 0)),
            pl.BlockSpec((1, D), const2),
            pl.BlockSpec((D, TN_IN), lambda i, j: (0, j)),
            pl.BlockSpec((1, TN_IN), lambda i, j: (0, j)),
            pl.BlockSpec((1, HEAD_DIM), const2),
            pl.BlockSpec((1, HEAD_DIM), const2),
            rope_spec, rope_spec, rope_spec,
        ],
        out_specs=pl.BlockSpec((TM_PROJ, TN_IN), lambda i, j: (i, j)),
        out_shape=jax.ShapeDtypeStruct((T, in_cols), BF16),
        scratch_shapes=[pltpu.VMEM((TM_PROJ, D), BF16), pltpu.VMEM((TM_PROJ, D), BF16)],
        compiler_params=_params("arbitrary", "arbitrary"),
        name="inproj",
    )(xt, _row(norm_mix_g[0]), w_in[0].astype(BF16), _row(b_in[0]),
      _row(q_norm_g[0]) * HEAD_DIM ** -0.5, _row(k_norm_g[0]), rope_c, rope_a, rope_b)

    gw = GQA_GROUP * HEAD_DIM
    k_col0 = attn_q // HEAD_DIM
    v_col0 = (attn_q + attn_kv) // HEAD_DIM
    attn = pl.pallas_call(
        _attn_kernel,
        grid=(B, N_KV_HEADS),
        in_specs=[
            pl.BlockSpec((S, gw), lambda b, h: (b, h)),
            pl.BlockSpec((S, HEAD_DIM), lambda b, h: (b, k_col0 + h)),
            pl.BlockSpec((S, HEAD_DIM), lambda b, h: (b, v_col0 + h)),
        ],
        out_specs=pl.BlockSpec((S, gw), lambda b, h: (b, h)),
        out_shape=jax.ShapeDtypeStruct((T, attn_q), BF16),
        compiler_params=_params("parallel", "parallel"),
        name="attn",
    )(z, z, z)

    conv = pl.pallas_call(
        _conv_kernel,
        grid=(B,),
        in_specs=[
            pl.BlockSpec((S, CONV_CH), lambda b: (b, u_col // CONV_CH)),
            pl.BlockSpec((S, CONV_CH), lambda b: (b, u_col // CONV_CH + 1)),
            pl.BlockSpec(conv_dw_w.shape[1:], lambda b: (0, 0)),
            pl.BlockSpec((1, CONV_CH), lambda b: (0, 0)),
            pl.BlockSpec((1, CONV_CH), lambda b: (0, 0)),
            pl.BlockSpec((1, CONV_CH), lambda b: (0, 0)),
        ],
        out_specs=pl.BlockSpec((S, CONV_CH), lambda b: (b, 0)),
        out_shape=jax.ShapeDtypeStruct((T, CONV_CH), BF16),
        scratch_shapes=[pltpu.VMEM((S + 2 * HALO, CONV_CH), F32),
                        pltpu.VMEM((S, CONV_CH), F32)],
        compiler_params=_params("parallel"),
        name="conv",
    )(z, z, conv_dw_w[0].astype(F32), _row(conv_dw_b[0]), _row(conv_ln_g[0]), _row(conv_ln_b[0]))

    z_blk = D
    once = pl.Buffered(1)
    x1, h2 = pl.pallas_call(
        _mixout_kernel,
        grid=(T // TM_MIX,),
        in_specs=[
            pl.BlockSpec((TM_MIX, attn_q), lambda i: (i, 0)),
            pl.BlockSpec((TM_MIX, CONV_CH), lambda i: (i, 0)),
            pl.BlockSpec((TM_MIX, z_blk), lambda i: (i, gate_col // z_blk)),
            pl.BlockSpec((TM_MIX, z_blk), lambda i: (i, gate_col // z_blk + 1)),
            pl.BlockSpec((TM_MIX, D), lambda i: (i, 0)),
            pl.BlockSpec((attn_q, D), lambda i: (0, 0), pipeline_mode=once),
            pl.BlockSpec((CONV_CH, D), lambda i: (0, 0), pipeline_mode=once),
            pl.BlockSpec((D, D), lambda i: (0, 0), pipeline_mode=once),
            pl.BlockSpec((1, D), lambda i: (0, 0)),
        ],
        out_specs=[pl.BlockSpec((TM_MIX, D), lambda i: (i, 0)),
                   pl.BlockSpec((TM_MIX, D), lambda i: (i, 0))],
        out_shape=[jax.ShapeDtypeStruct((T, D), F32), jax.ShapeDtypeStruct((T, D), BF16)],
        scratch_shapes=[pltpu.VMEM((TM_MIX, D), BF16)],
        compiler_params=_params("parallel"),
        name="mixout",
    )(attn, conv, z, z, xt, w_attn_o[0].astype(BF16), w_conv_o[0].astype(BF16),
      w_out[0].astype(BF16), _row(norm_ffn_g[0]))

    n_ft = T // TM_FFN
    hbf = TM_FFN // HALO
    n_hb = T // HALO
    n_f = ffn_dim // TF_FFN
    n_up = n_ft * n_f
    assert n_f >= 2
    w_up = w_ffn_up[0].astype(BF16)
    up_tile = lambda s: jnp.minimum(s // n_f, n_ft - 1)
    up_f = lambda s: jnp.minimum(s, n_up - 1) % n_f
    down_step = lambda s: jnp.maximum(s - 1, 0)
    y = pl.pallas_call(
        functools.partial(_ffn_kernel, n_f, n_ft, S // TM_FFN),
        grid=(n_up + 1,),
        in_specs=[
            pl.BlockSpec((HALO, D), lambda s: (jnp.maximum(up_tile(s) * hbf - 1, 0), 0)),
            pl.BlockSpec((TM_FFN, D), lambda s: (up_tile(s), 0)),
            pl.BlockSpec((HALO, D), lambda s: (jnp.minimum((up_tile(s) + 1) * hbf, n_hb - 1), 0)),
            pl.BlockSpec((TM_FFN, D), lambda s: (down_step(s) // n_f, 0)),
            pl.BlockSpec((D, TF_FFN), lambda s: (0, up_f(s))),
            pl.BlockSpec((D, TF_FFN), lambda s: (0, n_f + up_f(s))),
            pl.BlockSpec((ffn_dw_w.shape[1], TF_FFN), lambda s: (0, up_f(s))),
            pl.BlockSpec((1, TF_FFN), lambda s: (0, up_f(s))),
            pl.BlockSpec((TF_FFN, D), lambda s: (down_step(s) % n_f, 0)),
            pl.BlockSpec((1, D), lambda s: (0, 0)),
        ],
        out_specs=pl.BlockSpec((TM_FFN, D), lambda s: (down_step(s) // n_f, 0)),
        out_shape=jax.ShapeDtypeStruct((T, D), F32),
        scratch_shapes=[
            pltpu.VMEM((TM_FFN + 2 * HALO, D), BF16),
            pltpu.VMEM((TM_FFN + 2 * HALO, TF_FFN), F32),
            pltpu.VMEM((TM_FFN, TF_FFN), BF16),
            pltpu.VMEM((TM_FFN, TF_FFN), BF16),
            pltpu.VMEM((TM_FFN, D), F32),
        ],
        compiler_params=_params("arbitrary"),
        name="ffn",
    )(h2, h2, h2, x1, w_up, w_up, ffn_dw_w[0].astype(F32), _row(ffn_dw_b[0]),
      w_ffn_down[0].astype(BF16), _row(norm_final_g))
    return y.reshape(B, S, D)
```

```python
import functools

import jax
import jax.numpy as jnp
from jax import lax
from jax.experimental import pallas as pl
from jax.experimental.pallas import tpu as pltpu

F32 = jnp.float32
BF16 = jnp.bfloat16

EPS = 1e-6
HEAD_DIM = 128
N_Q_HEADS = 8
N_KV_HEADS = 4
GQA_GROUP = N_Q_HEADS // N_KV_HEADS
ROPE_THETA = 10000.0
GRID_W = 64
CONV_CH = 1024

V7X_VMEM_BYTES = 64 * 1024 * 1024
VMEM_LIMIT = V7X_VMEM_BYTES - 8 * 1024 * 1024
SUBLANES = 8
LANES = 128
BF16_SUBLANES = 16

TM_PROJ = 1024
TN_IN = 1024
TM_MIX = 512
TN_PROJ = 512
PRO_CHUNKS = 4
QK_GROUP_W = 256
TQ_ATTN = 512
GLU_ROWS = 128
CONV_UNROLL = 8
LN_ROWS = 128
TM_FFN = 512
TF_FFN = 512
HALO = BF16_SUBLANES


def _params(*sem):
    return pltpu.CompilerParams(dimension_semantics=sem, vmem_limit_bytes=VMEM_LIMIT)


def _rms_rows(x, g):
    ms = jnp.mean(x * x, axis=-1, keepdims=True)
    return x * lax.rsqrt(ms + EPS) * g


def _sigmoid(x):
    return 0.5 * jnp.tanh(0.5 * x) + 0.5


def _inproj_kernel(x_ref, g_ref, w_ref, b_ref, qg_ref, kg_ref, rc_ref, ra_ref, rb_ref, z_ref,
                   qk_ref, ha_ref, hb_ref, raw_ref):
    i = pl.program_id(0)
    j = pl.program_id(1)
    tm, tn = z_ref.shape
    group_w = qk_ref.shape[1]
    heads_per_group = group_w // HEAD_DIM
    n_groups = (N_Q_HEADS + N_KV_HEADS) * HEAD_DIM // group_w
    q_groups = N_Q_HEADS * HEAD_DIM // group_w
    raw_steps = -(-n_groups * group_w // tn)
    assert raw_steps + n_groups <= pl.num_programs(1) and tn % group_w == 0

    @pl.when((i == 0) & (j == 0))
    def _():
        ha_ref[...] = _rms_rows(x_ref[...], g_ref[...]).astype(BF16)
        raw_ref[...] = jnp.zeros_like(raw_ref)

    rows = tm // PRO_CHUNKS
    r0 = pl.multiple_of(jnp.clip(j - 1, 0, PRO_CHUNKS - 1) * rows, rows)
    grp = jnp.clip(j - raw_steps, 0, n_groups - 1)

    def step(h_cur, h_next):
        h_next[pl.ds(r0, rows), :] = _rms_rows(x_ref[pl.ds(r0, rows), :], g_ref[...]).astype(BF16)

        gain = jnp.where(grp < q_groups, qg_ref[...], kg_ref[...])
        raw = raw_ref[grp]
        rc, ra, rb = rc_ref[...], ra_ref[...], rb_ref[...]
        for hh in range(heads_per_group):
            sl = slice(hh * HEAD_DIM, (hh + 1) * HEAD_DIM)
            y = _rms_rows(raw[:, sl], gain)
            y = (y * rc + pltpu.roll(y, HEAD_DIM - 1, axis=1) * ra
                 + pltpu.roll(y, 1, axis=1) * rb)
            qk_ref[:, sl] = y.astype(BF16)

        z = jnp.dot(h_cur[...], w_ref[...], preferred_element_type=F32) + b_ref[...]
        z_ref[...] = z.astype(BF16)
        for js in range(raw_steps):
            @pl.when(j == js)
            def _(js=js):
                for gi in range(js * tn // group_w, min((js + 1) * tn // group_w, n_groups)):
                    c0 = gi * group_w - js * tn
                    raw_ref[gi] = z[:, c0:c0 + group_w]

    @pl.when(i % 2 == 0)
    def _():
        step(ha_ref, hb_ref)

    @pl.when(i % 2 == 1)
    def _():
        step(hb_ref, ha_ref)


def _attn_kernel(q_ref, k_ref, v_ref, o_ref):
    k = k_ref[...]
    v = v_ref[...]
    for g in range(GQA_GROUP):
        sl = slice(g * HEAD_DIM, (g + 1) * HEAD_DIM)
        for qb in range(q_ref.shape[0] // TQ_ATTN):
            rows = slice(qb * TQ_ATTN, (qb + 1) * TQ_ATTN)
            s = lax.dot_general(q_ref[rows, sl], k, (((1,), (1,)), ((), ())),
                                preferred_element_type=F32)
            p = jnp.exp(s - jnp.max(s, axis=-1, keepdims=True))
            l = jnp.sum(p, axis=-1, keepdims=True)
            o = jnp.dot(p.astype(BF16), v, preferred_element_type=F32)
            o_ref[rows, sl] = (o / l).astype(BF16)


def _conv_kernel(u_ref, ug_ref, w_ref, b_ref, lg_ref, lb_ref, o_ref, buf_ref, conv_ref):
    ts = u_ref.shape[0]
    n_taps = w_ref.shape[0]
    pad = n_taps // 2
    buf_ref[0:HALO, :] = jnp.zeros((HALO, buf_ref.shape[1]), F32)
    buf_ref[HALO + ts:, :] = jnp.zeros((HALO, buf_ref.shape[1]), F32)

    def glu(r, carry):
        r0 = pl.multiple_of(r * GLU_ROWS, GLU_ROWS)
        u = u_ref[pl.ds(r0, GLU_ROWS), :].astype(F32)
        ug = ug_ref[pl.ds(r0, GLU_ROWS), :].astype(F32)
        buf_ref[pl.ds(HALO + r0, GLU_ROWS), :] = u * _sigmoid(ug)
        return carry

    lax.fori_loop(0, ts // GLU_ROWS, glu, 0)

    a_lo, a_hi = -((pad + SUBLANES - 1) // SUBLANES), pad // SUBLANES
    halo_groups = HALO // SUBLANES
    assert -a_lo <= halo_groups and a_hi + 1 <= halo_groups
    sub = lax.broadcasted_iota(jnp.int32, (SUBLANES, LANES), 0)

    for lane_group in range(buf_ref.shape[1] // LANES):
        ls = slice(lane_group * LANES, (lane_group + 1) * LANES)
        wv = [jnp.broadcast_to(w_ref[t:t + 1, ls], (SUBLANES, LANES)) for t in range(n_taps)]
        bias = jnp.broadcast_to(b_ref[:, ls], (SUBLANES, LANES))

        def load_group(gi, ls=ls):
            return buf_ref[pl.ds(pl.multiple_of(gi * SUBLANES, SUBLANES), SUBLANES), ls]

        def partial_sums(xs, wv=wv):
            ys = []
            for b in range(SUBLANES):
                y = None
                for ai, a in enumerate(range(a_lo, a_hi + 1)):
                    t = pad + SUBLANES * a + b
                    if 0 <= t < n_taps:
                        term = xs[ai] * wv[t]
                        y = term if y is None else y + term
                ys.append(y)
            return ys

        def body(g, carry, ls=ls, bias=bias, load_group=load_group, partial_sums=partial_sums):
            y_prev = carry[:SUBLANES]
            xs = list(carry[SUBLANES:]) + [load_group(g + 1 + a_hi + halo_groups)]
            y_new = partial_sums(xs)
            r = None
            for b in range(SUBLANES - 1, -1, -1):
                cb = y_prev[0] if b == 0 else jnp.where(sub >= b, y_prev[b], y_new[b])
                r = cb if r is None else cb + pltpu.roll(r, SUBLANES - 1, axis=0)
            conv_ref[pl.ds(pl.multiple_of(g * SUBLANES, SUBLANES), SUBLANES), ls] = r + bias
            return tuple(y_new) + tuple(xs[1:])

        x0 = [load_group(halo_groups + a_lo + i) for i in range(a_hi - a_lo + 1)]
        lax.fori_loop(0, ts // SUBLANES, body, tuple(partial_sums(x0)) + tuple(x0[1:]),
                      unroll=CONV_UNROLL)

    def layer_norm(r, carry):
        r0 = pl.multiple_of(r * LN_ROWS, LN_ROWS)
        acc = conv_ref[pl.ds(r0, LN_ROWS), :]
        mu = jnp.mean(acc, axis=-1, keepdims=True)
        d = acc - mu
        var = jnp.mean(d * d, axis=-1, keepdims=True)
        y = d * lax.rsqrt(var + EPS) * lg_ref[...] + lb_ref[...]
        o_ref[pl.ds(r0, LN_ROWS), :] = (y * _sigmoid(y)).astype(BF16)
        return carry

    lax.fori_loop(0, ts // LN_ROWS, layer_norm, 0)


def _mixout_kernel(a_ref, c_ref, ga_ref, gc_ref, x_ref, wa_ref, wc_ref, wo_ref, fg_ref,
                   x1_ref, h2_ref, m_ref):
    a = a_ref[...]
    c = c_ref[...]
    n_chunks = x_ref.shape[1] // TN_PROJ
    for jc in range(n_chunks):
        cs = slice(jc * TN_PROJ, (jc + 1) * TN_PROJ)
        pa = jnp.dot(a, wa_ref[:, cs], preferred_element_type=F32)
        pc = jnp.dot(c, wc_ref[:, cs], preferred_element_type=F32)
        ga = _sigmoid(ga_ref[:, cs].astype(F32))
        gc = _sigmoid(gc_ref[:, cs].astype(F32))
        m_ref[:, cs] = (ga * pa + gc * pc).astype(BF16)
    m = m_ref[...]
    for jc in range(n_chunks):
        cs = slice(jc * TN_PROJ, (jc + 1) * TN_PROJ)
        x1_ref[:, cs] = x_ref[:, cs] + jnp.dot(m, wo_ref[:, cs], preferred_element_type=F32)
    h2_ref[...] = _rms_rows(x1_ref[...], fg_ref[...]).astype(BF16)


def _ffn_kernel(n_f, n_tiles, n_seq_tiles, prev_ref, cur_ref, next_ref, res_ref, wg_ref, wv_ref,
                cw_ref, cb_ref, wd_ref, fg_ref, o_ref, h_ref, gt_ref, acta_ref, actb_ref, acc_ref):
    s = pl.program_id(0)
    n_up = n_f * n_tiles
    f_up = s % n_f
    si = (s // n_f) % n_seq_tiles
    tm = cur_ref.shape[0]

    def load_tile():
        prev = prev_ref[...]
        nxt = next_ref[...]
        h_ref[0:HALO, :] = jnp.where(si > 0, prev, jnp.zeros_like(prev))
        h_ref[HALO:HALO + tm, :] = cur_ref[...]
        h_ref[HALO + tm:, :] = jnp.where(si < n_seq_tiles - 1, nxt, jnp.zeros_like(nxt))

    def up(act_ref):
        gt_ref[...] = jnp.dot(h_ref[...], wg_ref[...], preferred_element_type=F32)
        val = jnp.dot(h_ref[HALO:HALO + tm, :], wv_ref[...], preferred_element_type=F32)
        gt = (gt_ref[HALO - 1:HALO - 1 + tm, :] * cw_ref[0:1, :]
              + gt_ref[HALO:HALO + tm, :] * cw_ref[1:2, :]
              + gt_ref[HALO + 1:HALO + 1 + tm, :] * cw_ref[2:3, :]
              + cb_ref[...])
        act_ref[...] = (gt * _sigmoid(gt) * val).astype(BF16)

    def down(act_ref, first):
        d = jnp.dot(act_ref[...], wd_ref[...], preferred_element_type=F32)
        if first:
            acc_ref[...] = d
        else:
            acc_ref[...] += d

    def finish_tile():
        o_ref[...] = _rms_rows(res_ref[...] + acc_ref[...], fg_ref[...])

    acts = (acta_ref, actb_ref)
    for parity in (0, 1):
        a_new, a_old = acts[parity], acts[1 - parity]
        mine = (s % 2) == parity

        @pl.when(mine & (s == 0))
        def _():
            load_tile()
            up(a_new)

        @pl.when(mine & (f_up == 0) & (s > 0) & (s < n_up))
        def _():
            down(a_old, False)
            finish_tile()
            load_tile()
            up(a_new)

        @pl.when(mine & (f_up == 1))
        def _():
            up(a_new)
            down(a_old, True)

        @pl.when(mine & (f_up >= 2))
        def _():
            up(a_new)
            down(a_old, False)

        @pl.when(mine & (s == n_up))
        def _():
            down(a_old, False)
            finish_tile()


def _rope_tables(seq_len):
    rows = seq_len // GRID_W
    half = HEAD_DIM // 2
    row = jnp.repeat(jnp.arange(rows), GRID_W).astype(F32)
    col = jnp.tile(jnp.arange(GRID_W), rows).astype(F32)
    inv_freq = ROPE_THETA ** (-jnp.arange(0, half, 2, dtype=F32) / half)
    ang = jnp.concatenate([row[:, None] * inv_freq, col[:, None] * inv_freq], axis=-1)
    cos = jnp.repeat(jnp.cos(ang), 2, axis=-1)
    sin = jnp.repeat(jnp.sin(ang), 2, axis=-1)
    even = (jnp.arange(HEAD_DIM) % 2 == 0)[None, :]
    return cos, jnp.where(even, -sin, 0.0), jnp.where(even, 0.0, sin)


def _row(v):
    return v.reshape(1, -1).astype(F32)


def kernel(x, norm_mix_g, w_in, b_in, q_norm_g, k_norm_g, w_attn_o, conv_dw_w, conv_dw_b,
           conv_ln_g, conv_ln_b, w_conv_o, w_out, norm_ffn_g, w_ffn_up, ffn_dw_w, ffn_dw_b,
           w_ffn_down, norm_final_g):
    B, S, D = x.shape
    T = B * S
    assert w_in.shape[0] == 1, "single trunk layer: the final rmsnorm is fused into the FFN"
    attn_q = N_Q_HEADS * HEAD_DIM
    attn_kv = N_KV_HEADS * HEAD_DIM
    in_cols = w_in.shape[2]
    ffn_dim = w_ffn_down.shape[1]
    u_col = attn_q + 2 * attn_kv
    gate_col = u_col + 2 * CONV_CH
    assert in_cols == gate_col + 2 * D
    assert T % TM_PROJ == 0 and S % TM_PROJ == 0 and in_cols % TN_IN == 0
    assert attn_q % TN_IN == 0 and u_col % CONV_CH == 0 and D % TN_PROJ == 0
    assert S % TQ_ATTN == 0 and S % TM_FFN == 0 and ffn_dim % TF_FFN == 0 and T % TM_MIX == 0

    rope_c, rope_a, rope_b = _rope_tables(S)
    xt = x.reshape(T, D)
    n_i = T // TM_PROJ
    seq_tiles = S // TM_PROJ

    const2 = lambda i, j: (0, 0)
    rope_spec = pl.BlockSpec((TM_PROJ, HEAD_DIM), lambda i, j: (i % seq_tiles, 0))
    n_j = in_cols // TN_IN
    qk_cols = attn_q + attn_kv
    n_groups = qk_cols // QK_GROUP_W
    raw_steps = -(-qk_cols // TN_IN)
    w_in_tiles = w_in[0].astype(BF16).reshape(D, n_j, TN_IN).transpose(1, 0, 2)
    z, qk = pl.pallas_call(
        _inproj_kernel,
        grid=(n_i, n_j),
        in_specs=[
            pl.BlockSpec((TM_PROJ, D), lambda i, j: (
                jnp.where((i == 0) & (j == 0), 0, jnp.minimum(i + 1, n_i - 1)), 0)),
            pl.BlockSpec((1, D), const2),
            pl.BlockSpec((None, D, TN_IN), lambda i, j: (j, 0, 0)),
            pl.BlockSpec((1, TN_IN), lambda i, j: (0, j)),
            pl.BlockSpec((1, HEAD_DIM), const2),
            pl.BlockSpec((1, HEAD_DIM), const2),
            rope_spec, rope_spec, rope_spec,
        ],
        out_specs=[
            pl.BlockSpec((TM_PROJ, TN_IN), lambda i, j: (i, j)),
            pl.BlockSpec((TM_PROJ, QK_GROUP_W),
                         lambda i, j: (i, jnp.clip(j - raw_steps, 0, n_groups - 1))),
        ],
        out_shape=[jax.ShapeDtypeStruct((T, in_cols), BF16),
                   jax.ShapeDtypeStruct((T, qk_cols), BF16)],
        scratch_shapes=[pltpu.VMEM((TM_PROJ, D), BF16), pltpu.VMEM((TM_PROJ, D), BF16),
                        pltpu.VMEM((n_groups, TM_PROJ, QK_GROUP_W), F32)],
        compiler_params=_params("arbitrary", "arbitrary"),
        name="inproj",
    )(xt, _row(norm_mix_g[0]), w_in_tiles, _row(b_in[0]),
      _row(q_norm_g[0]) * HEAD_DIM ** -0.5, _row(k_norm_g[0]), rope_c, rope_a, rope_b)

    gw = GQA_GROUP * HEAD_DIM
    k_col0 = attn_q // HEAD_DIM
    v_col0 = (attn_q + attn_kv) // HEAD_DIM
    attn = pl.pallas_call(
        _attn_kernel,
        grid=(B, N_KV_HEADS),
        in_specs=[
            pl.BlockSpec((S, gw), lambda b, h: (b, h)),
            pl.BlockSpec((S, HEAD_DIM), lambda b, h: (b, k_col0 + h)),
            pl.BlockSpec((S, HEAD_DIM), lambda b, h: (b, v_col0 + h)),
        ],
        out_specs=pl.BlockSpec((S, gw), lambda b, h: (b, h)),
        out_shape=jax.ShapeDtypeStruct((T, attn_q), BF16),
        compiler_params=_params("parallel", "parallel"),
        name="attn",
    )(qk, qk, z)

    conv = pl.pallas_call(
        _conv_kernel,
        grid=(B,),
        in_specs=[
            pl.BlockSpec((S, CONV_CH), lambda b: (b, u_col // CONV_CH)),
            pl.BlockSpec((S, CONV_CH), lambda b: (b, u_col // CONV_CH + 1)),
            pl.BlockSpec(conv_dw_w.shape[1:], lambda b: (0, 0)),
            pl.BlockSpec((1, CONV_CH), lambda b: (0, 0)),
            pl.BlockSpec((1, CONV_CH), lambda b: (0, 0)),
            pl.BlockSpec((1, CONV_CH), lambda b: (0, 0)),
        ],
        out_specs=pl.BlockSpec((S, CONV_CH), lambda b: (b, 0)),
        out_shape=jax.ShapeDtypeStruct((T, CONV_CH), BF16),
        scratch_shapes=[pltpu.VMEM((S + 2 * HALO, CONV_CH), F32),
                        pltpu.VMEM((S, CONV_CH), F32)],
        compiler_params=_params("parallel"),
        name="conv",
    )(z, z, conv_dw_w[0].astype(F32), _row(conv_dw_b[0]), _row(conv_ln_g[0]), _row(conv_ln_b[0]))

    z_blk = D
    once = pl.Buffered(1)
    x1, h2 = pl.pallas_call(
        _mixout_kernel,
        grid=(T // TM_MIX,),
        in_specs=[
            pl.BlockSpec((TM_MIX, attn_q), lambda i: (i, 0)),
            pl.BlockSpec((TM_MIX, CONV_CH), lambda i: (i, 0)),
            pl.BlockSpec((TM_MIX, z_blk), lambda i: (i, gate_col // z_blk)),
            pl.BlockSpec((TM_MIX, z_blk), lambda i: (i, gate_col // z_blk + 1)),
            pl.BlockSpec((TM_MIX, D), lambda i: (i, 0)),
            pl.BlockSpec((attn_q, D), lambda i: (0, 0), pipeline_mode=once),
            pl.BlockSpec((CONV_CH, D), lambda i: (0, 0), pipeline_mode=once),
            pl.BlockSpec((D, D), lambda i: (0, 0), pipeline_mode=once),
            pl.BlockSpec((1, D), lambda i: (0, 0)),
        ],
        out_specs=[pl.BlockSpec((TM_MIX, D), lambda i: (i, 0)),
                   pl.BlockSpec((TM_MIX, D), lambda i: (i, 0))],
        out_shape=[jax.ShapeDtypeStruct((T, D), F32), jax.ShapeDtypeStruct((T, D), BF16)],
        scratch_shapes=[pltpu.VMEM((TM_MIX, D), BF16)],
        compiler_params=_params("parallel"),
        name="mixout",
    )(attn, conv, z, z, xt, w_attn_o[0].astype(BF16), w_conv_o[0].astype(BF16),
      w_out[0].astype(BF16), _row(norm_ffn_g[0]))

    n_ft = T // TM_FFN
    hbf = TM_FFN // HALO
    n_hb = T // HALO
    n_f = ffn_dim // TF_FFN
    n_up = n_ft * n_f
    assert n_f >= 2
    w_up = w_ffn_up[0].astype(BF16).reshape(D, 2 * n_f, TF_FFN).transpose(1, 0, 2)
    up_tile = lambda s: jnp.minimum(s // n_f, n_ft - 1)
    up_f = lambda s: jnp.minimum(s, n_up - 1) % n_f
    down_step = lambda s: jnp.maximum(s - 1, 0)
    y = pl.pallas_call(
        functools.partial(_ffn_kernel, n_f, n_ft, S // TM_FFN),
        grid=(n_up + 1,),
        in_specs=[
            pl.BlockSpec((HALO, D), lambda s: (jnp.maximum(up_tile(s) * hbf - 1, 0), 0)),
            pl.BlockSpec((TM_FFN, D), lambda s: (up_tile(s), 0)),
            pl.BlockSpec((HALO, D), lambda s: (jnp.minimum((up_tile(s) + 1) * hbf, n_hb - 1), 0)),
            pl.BlockSpec((TM_FFN, D), lambda s: (down_step(s) // n_f, 0)),
            pl.BlockSpec((None, D, TF_FFN), lambda s: (up_f(s), 0, 0)),
            pl.BlockSpec((None, D, TF_FFN), lambda s: (n_f + up_f(s), 0, 0)),
            pl.BlockSpec((ffn_dw_w.shape[1], TF_FFN), lambda s: (0, up_f(s))),
            pl.BlockSpec((1, TF_FFN), lambda s: (0, up_f(s))),
            pl.BlockSpec((TF_FFN, D), lambda s: (down_step(s) % n_f, 0)),
            pl.BlockSpec((1, D), lambda s: (0, 0)),
        ],
        out_specs=pl.BlockSpec((TM_FFN, D), lambda s: (down_step(s) // n_f, 0)),
        out_shape=jax.ShapeDtypeStruct((T, D), F32),
        scratch_shapes=[
            pltpu.VMEM((TM_FFN + 2 * HALO, D), BF16),
            pltpu.VMEM((TM_FFN + 2 * HALO, TF_FFN), F32),
            pltpu.VMEM((TM_FFN, TF_FFN), BF16),
            pltpu.VMEM((TM_FFN, TF_FFN), BF16),
            pltpu.VMEM((TM_FFN, D), F32),
        ],
        compiler_params=_params("arbitrary"),
        name="ffn",
    )(h2, h2, h2, x1, w_up, w_up, ffn_dw_w[0].astype(F32), _row(ffn_dw_b[0]),
      w_ffn_down[0].astype(BF16), _row(norm_final_g))
    return y.reshape(B, S, D)
```

```python
import functools

import jax
import jax.numpy as jnp
from jax import lax
from jax.experimental import pallas as pl
from jax.experimental.pallas import tpu as pltpu

F32 = jnp.float32
BF16 = jnp.bfloat16

EPS = 1e-6
HEAD_DIM = 128
N_Q_HEADS = 8
N_KV_HEADS = 4
GQA_GROUP = N_Q_HEADS // N_KV_HEADS
ROPE_THETA = 10000.0
GRID_W = 64
CONV_CH = 1024

V7X_VMEM_BYTES = 64 * 1024 * 1024
VMEM_LIMIT = V7X_VMEM_BYTES - 8 * 1024 * 1024
SUBLANES = 8
LANES = 128
BF16_SUBLANES = 16

TM_PROJ = 1024
TN_IN = 1024
TM_MIX = 512
TN_PROJ = 512
PRO_CHUNKS = 4
QK_GROUP_W = 256
TQ_ATTN = 512
GLU_ROWS = 128
CONV_UNROLL = 8
LN_ROWS = 128
TM_FFN = 1024
TF_FFN = 512
HALO = BF16_SUBLANES


def _params(*sem):
    return pltpu.CompilerParams(dimension_semantics=sem, vmem_limit_bytes=VMEM_LIMIT)


def _rms_rows(x, g):
    ms = jnp.mean(x * x, axis=-1, keepdims=True)
    return x * lax.rsqrt(ms + EPS) * g


def _sigmoid(x):
    return 0.5 * jnp.tanh(0.5 * x) + 0.5


def _inproj_kernel(x_ref, g_ref, w_ref, b_ref, qg_ref, kg_ref, rc_ref, ra_ref, rb_ref, z_ref,
                   qk_ref, ha_ref, hb_ref, raw_ref):
    i = pl.program_id(0)
    j = pl.program_id(1)
    tm, tn = z_ref.shape
    group_w = qk_ref.shape[1]
    heads_per_group = group_w // HEAD_DIM
    n_groups = (N_Q_HEADS + N_KV_HEADS) * HEAD_DIM // group_w
    q_groups = N_Q_HEADS * HEAD_DIM // group_w
    raw_steps = -(-n_groups * group_w // tn)
    assert raw_steps + n_groups <= pl.num_programs(1) and tn % group_w == 0

    @pl.when((i == 0) & (j == 0))
    def _():
        ha_ref[...] = _rms_rows(x_ref[...], g_ref[...]).astype(BF16)
        raw_ref[...] = jnp.zeros_like(raw_ref)

    rows = tm // PRO_CHUNKS
    r0 = pl.multiple_of(jnp.clip(j - 1, 0, PRO_CHUNKS - 1) * rows, rows)
    grp = jnp.clip(j - raw_steps, 0, n_groups - 1)

    def step(h_cur, h_next):
        h_next[pl.ds(r0, rows), :] = _rms_rows(x_ref[pl.ds(r0, rows), :], g_ref[...]).astype(BF16)

        gain = jnp.where(grp < q_groups, qg_ref[...], kg_ref[...])
        raw = raw_ref[grp]
        rc, ra, rb = rc_ref[...], ra_ref[...], rb_ref[...]
        for hh in range(heads_per_group):
            sl = slice(hh * HEAD_DIM, (hh + 1) * HEAD_DIM)
            y = _rms_rows(raw[:, sl], gain)
            y = (y * rc + pltpu.roll(y, HEAD_DIM - 1, axis=1) * ra
                 + pltpu.roll(y, 1, axis=1) * rb)
            qk_ref[:, sl] = y.astype(BF16)

        z = jnp.dot(h_cur[...], w_ref[...], preferred_element_type=F32) + b_ref[...]
        z_ref[...] = z.astype(BF16)
        for js in range(raw_steps):
            @pl.when(j == js)
            def _(js=js):
                for gi in range(js * tn // group_w, min((js + 1) * tn // group_w, n_groups)):
                    c0 = gi * group_w - js * tn
                    raw_ref[gi] = z[:, c0:c0 + group_w]

    @pl.when(i % 2 == 0)
    def _():
        step(ha_ref, hb_ref)

    @pl.when(i % 2 == 1)
    def _():
        step(hb_ref, ha_ref)


def _attn_kernel(q_ref, k_ref, v_ref, o_ref):
    k = k_ref[...]
    v = v_ref[...]
    for g in range(GQA_GROUP):
        sl = slice(g * HEAD_DIM, (g + 1) * HEAD_DIM)
        for qb in range(q_ref.shape[0] // TQ_ATTN):
            rows = slice(qb * TQ_ATTN, (qb + 1) * TQ_ATTN)
            s = lax.dot_general(q_ref[rows, sl], k, (((1,), (1,)), ((), ())),
                                preferred_element_type=F32)
            p = jnp.exp(s - jnp.max(s, axis=-1, keepdims=True))
            l = jnp.sum(p, axis=-1, keepdims=True)
            o = jnp.dot(p.astype(BF16), v, preferred_element_type=F32)
            o_ref[rows, sl] = (o / l).astype(BF16)


def _conv_kernel(u_ref, ug_ref, w_ref, b_ref, lg_ref, lb_ref, o_ref, buf_ref, conv_ref):
    ts = u_ref.shape[0]
    n_taps = w_ref.shape[0]
    pad = n_taps // 2
    buf_ref[0:HALO, :] = jnp.zeros((HALO, buf_ref.shape[1]), F32)
    buf_ref[HALO + ts:, :] = jnp.zeros((HALO, buf_ref.shape[1]), F32)

    def glu(r, carry):
        r0 = pl.multiple_of(r * GLU_ROWS, GLU_ROWS)
        u = u_ref[pl.ds(r0, GLU_ROWS), :].astype(F32)
        ug = ug_ref[pl.ds(r0, GLU_ROWS), :].astype(F32)
        buf_ref[pl.ds(HALO + r0, GLU_ROWS), :] = u * _sigmoid(ug)
        return carry

    lax.fori_loop(0, ts // GLU_ROWS, glu, 0)

    a_lo, a_hi = -((pad + SUBLANES - 1) // SUBLANES), pad // SUBLANES
    halo_groups = HALO // SUBLANES
    assert -a_lo <= halo_groups and a_hi + 1 <= halo_groups
    sub = lax.broadcasted_iota(jnp.int32, (SUBLANES, LANES), 0)

    for lane_group in range(buf_ref.shape[1] // LANES):
        ls = slice(lane_group * LANES, (lane_group + 1) * LANES)
        wv = [jnp.broadcast_to(w_ref[t:t + 1, ls], (SUBLANES, LANES)) for t in range(n_taps)]
        bias = jnp.broadcast_to(b_ref[:, ls], (SUBLANES, LANES))

        def load_group(gi, ls=ls):
            return buf_ref[pl.ds(pl.multiple_of(gi * SUBLANES, SUBLANES), SUBLANES), ls]

        def partial_sums(xs, wv=wv):
            ys = []
            for b in range(SUBLANES):
                y = None
                for ai, a in enumerate(range(a_lo, a_hi + 1)):
                    t = pad + SUBLANES * a + b
                    if 0 <= t < n_taps:
                        term = xs[ai] * wv[t]
                        y = term if y is None else y + term
                ys.append(y)
            return ys

        def body(g, carry, ls=ls, bias=bias, load_group=load_group, partial_sums=partial_sums):
            y_prev = carry[:SUBLANES]
            xs = list(carry[SUBLANES:]) + [load_group(g + 1 + a_hi + halo_groups)]
            y_new = partial_sums(xs)
            r = None
            for b in range(SUBLANES - 1, -1, -1):
                cb = y_prev[0] if b == 0 else jnp.where(sub >= b, y_prev[b], y_new[b])
                r = cb if r is None else cb + pltpu.roll(r, SUBLANES - 1, axis=0)
            conv_ref[pl.ds(pl.multiple_of(g * SUBLANES, SUBLANES), SUBLANES), ls] = r + bias
            return tuple(y_new) + tuple(xs[1:])

        x0 = [load_group(halo_groups + a_lo + i) for i in range(a_hi - a_lo + 1)]
        lax.fori_loop(0, ts // SUBLANES, body, tuple(partial_sums(x0)) + tuple(x0[1:]),
                      unroll=CONV_UNROLL)

    def layer_norm(r, carry):
        r0 = pl.multiple_of(r * LN_ROWS, LN_ROWS)
        acc = conv_ref[pl.ds(r0, LN_ROWS), :]
        mu = jnp.mean(acc, axis=-1, keepdims=True)
        d = acc - mu
        var = jnp.mean(d * d, axis=-1, keepdims=True)
        y = d * lax.rsqrt(var + EPS) * lg_ref[...] + lb_ref[...]
        o_ref[pl.ds(r0, LN_ROWS), :] = (y * _sigmoid(y)).astype(BF16)
        return carry

    lax.fori_loop(0, ts // LN_ROWS, layer_norm, 0)


def _mixout_kernel(a_ref, c_ref, ga_ref, gc_ref, x_ref, wa_ref, wc_ref, wo_ref, fg_ref,
                   x1_ref, h2_ref, m_ref):
    a = a_ref[...]
    c = c_ref[...]
    n_chunks = x_ref.shape[1] // TN_PROJ
    for jc in range(n_chunks):
        cs = slice(jc * TN_PROJ, (jc + 1) * TN_PROJ)
        pa = jnp.dot(a, wa_ref[:, cs], preferred_element_type=F32)
        pc = jnp.dot(c, wc_ref[:, cs], preferred_element_type=F32)
        ga = _sigmoid(ga_ref[:, cs].astype(F32))
        gc = _sigmoid(gc_ref[:, cs].astype(F32))
        m_ref[:, cs] = (ga * pa + gc * pc).astype(BF16)
    m = m_ref[...]
    for jc in range(n_chunks):
        cs = slice(jc * TN_PROJ, (jc + 1) * TN_PROJ)
        x1_ref[:, cs] = x_ref[:, cs] + jnp.dot(m, wo_ref[:, cs], preferred_element_type=F32)
    h2_ref[...] = _rms_rows(x1_ref[...], fg_ref[...]).astype(BF16)


def _ffn_kernel(n_f, n_tiles, n_seq_tiles, prev_ref, cur_ref, next_ref, x1_hbm, wg_ref, wv_ref,
                cw_ref, cb_ref, wd_ref, fg_ref, out_hbm, h_ref, gt_ref, acta_ref, actb_ref, acc_ref,
                y_ref, sem_ref):
    s = pl.program_id(0)
    n_up = n_f * n_tiles
    f_up = s % n_f
    tile_up = jnp.minimum(s // n_f, n_tiles - 1)
    tile_dn = jnp.maximum(s - 1, 0) // n_f
    si = tile_up % n_seq_tiles
    tm = cur_ref.shape[0]

    def tile_rows(ref, tile):
        return ref.at[pl.ds(pl.multiple_of(tile * tm, tm), tm), :]

    def res_copy(tile):
        return pltpu.make_async_copy(tile_rows(x1_hbm, tile), acc_ref, sem_ref.at[0])

    def out_copy(tile):
        return pltpu.make_async_copy(y_ref, tile_rows(out_hbm, tile), sem_ref.at[1])

    @pl.when(s == 0)
    def _():
        res_copy(0).start()

    @pl.when((f_up == 1) & (s < n_up))
    def _():
        res_copy(tile_up).wait()

    @pl.when((f_up == 1) & (s > n_f) & (s < n_up))
    def _():
        out_copy(tile_up - 1).wait()

    def load_tile():
        prev = prev_ref[...]
        nxt = next_ref[...]
        h_ref[0:HALO, :] = jnp.where(si > 0, prev, jnp.zeros_like(prev))
        h_ref[HALO:HALO + tm, :] = cur_ref[...]
        h_ref[HALO + tm:, :] = jnp.where(si < n_seq_tiles - 1, nxt, jnp.zeros_like(nxt))

    def up(act_ref):
        gt_ref[...] = jnp.dot(h_ref[...], wg_ref[...], preferred_element_type=F32)
        val = jnp.dot(h_ref[HALO:HALO + tm, :], wv_ref[...], preferred_element_type=F32)
        gt = (gt_ref[HALO - 1:HALO - 1 + tm, :] * cw_ref[0:1, :]
              + gt_ref[HALO:HALO + tm, :] * cw_ref[1:2, :]
              + gt_ref[HALO + 1:HALO + 1 + tm, :] * cw_ref[2:3, :]
              + cb_ref[...])
        act_ref[...] = (gt * _sigmoid(gt) * val).astype(BF16)

    def down(act_ref):
        acc_ref[...] += jnp.dot(act_ref[...], wd_ref[...], preferred_element_type=F32)

    def finish_tile():
        y_ref[...] = _rms_rows(acc_ref[...], fg_ref[...])
        out_copy(tile_dn).start()

    acts = (acta_ref, actb_ref)
    for parity in (0, 1):
        a_new, a_old = acts[parity], acts[1 - parity]
        mine = (s % 2) == parity

        @pl.when(mine & (s == 0))
        def _():
            load_tile()
            up(a_new)

        @pl.when(mine & (f_up == 0) & (s > 0) & (s < n_up))
        def _():
            down(a_old)
            finish_tile()
            res_copy(tile_up).start()
            load_tile()
            up(a_new)

        @pl.when(mine & (f_up >= 1))
        def _():
            up(a_new)
            down(a_old)

        @pl.when(mine & (s == n_up))
        def _():
            down(a_old)
            finish_tile()
            out_copy(tile_dn).wait()


def _rope_tables(seq_len):
    rows = seq_len // GRID_W
    half = HEAD_DIM // 2
    row = jnp.repeat(jnp.arange(rows), GRID_W).astype(F32)
    col = jnp.tile(jnp.arange(GRID_W), rows).astype(F32)
    inv_freq = ROPE_THETA ** (-jnp.arange(0, half, 2, dtype=F32) / half)
    ang = jnp.concatenate([row[:, None] * inv_freq, col[:, None] * inv_freq], axis=-1)
    cos = jnp.repeat(jnp.cos(ang), 2, axis=-1)
    sin = jnp.repeat(jnp.sin(ang), 2, axis=-1)
    even = (jnp.arange(HEAD_DIM) % 2 == 0)[None, :]
    return cos, jnp.where(even, -sin, 0.0), jnp.where(even, 0.0, sin)


def _row(v):
    return v.reshape(1, -1).astype(F32)


def kernel(x, norm_mix_g, w_in, b_in, q_norm_g, k_norm_g, w_attn_o, conv_dw_w, conv_dw_b,
           conv_ln_g, conv_ln_b, w_conv_o, w_out, norm_ffn_g, w_ffn_up, ffn_dw_w, ffn_dw_b,
           w_ffn_down, norm_final_g):
    B, S, D = x.shape
    T = B * S
    assert w_in.shape[0] == 1, "single trunk layer: the final rmsnorm is fused into the FFN"
    attn_q = N_Q_HEADS * HEAD_DIM
    attn_kv = N_KV_HEADS * HEAD_DIM
    in_cols = w_in.shape[2]
    ffn_dim = w_ffn_down.shape[1]
    u_col = attn_q + 2 * attn_kv
    gate_col = u_col + 2 * CONV_CH
    assert in_cols == gate_col + 2 * D
    assert T % TM_PROJ == 0 and S % TM_PROJ == 0 and in_cols % TN_IN == 0
    assert attn_q % TN_IN == 0 and u_col % CONV_CH == 0 and D % TN_PROJ == 0
    assert S % TQ_ATTN == 0 and S % TM_FFN == 0 and ffn_dim % TF_FFN == 0 and T % TM_MIX == 0

    rope_c, rope_a, rope_b = _rope_tables(S)
    xt = x.reshape(T, D)
    n_i = T // TM_PROJ
    seq_tiles = S // TM_PROJ

    const2 = lambda i, j: (0, 0)
    rope_spec = pl.BlockSpec((TM_PROJ, HEAD_DIM), lambda i, j: (i % seq_tiles, 0))
    n_j = in_cols // TN_IN
    qk_cols = attn_q + attn_kv
    n_groups = qk_cols // QK_GROUP_W
    raw_steps = -(-qk_cols // TN_IN)
    z, qk = pl.pallas_call(
        _inproj_kernel,
        grid=(n_i, n_j),
        in_specs=[
            pl.BlockSpec((TM_PROJ, D), lambda i, j: (
                jnp.where((i == 0) & (j == 0), 0, jnp.minimum(i + 1, n_i - 1)), 0)),
            pl.BlockSpec((1, D), const2),
            pl.BlockSpec((D, TN_IN), lambda i, j: (0, j)),
            pl.BlockSpec((1, TN_IN), lambda i, j: (0, j)),
            pl.BlockSpec((1, HEAD_DIM), const2),
            pl.BlockSpec((1, HEAD_DIM), const2),
            rope_spec, rope_spec, rope_spec,
        ],
        out_specs=[
            pl.BlockSpec((TM_PROJ, TN_IN), lambda i, j: (i, j)),
            pl.BlockSpec((TM_PROJ, QK_GROUP_W),
                         lambda i, j: (i, jnp.clip(j - raw_steps, 0, n_groups - 1))),
        ],
        out_shape=[jax.ShapeDtypeStruct((T, in_cols), BF16),
                   jax.ShapeDtypeStruct((T, qk_cols), BF16)],
        scratch_shapes=[pltpu.VMEM((TM_PROJ, D), BF16), pltpu.VMEM((TM_PROJ, D), BF16),
                        pltpu.VMEM((n_groups, TM_PROJ, QK_GROUP_W), F32)],
        compiler_params=_params("arbitrary", "arbitrary"),
        name="inproj",
    )(xt, _row(norm_mix_g[0]), w_in[0].astype(BF16), _row(b_in[0]),
      _row(q_norm_g[0]) * HEAD_DIM ** -0.5, _row(k_norm_g[0]), rope_c, rope_a, rope_b)

    gw = GQA_GROUP * HEAD_DIM
    k_col0 = attn_q // HEAD_DIM
    v_col0 = (attn_q + attn_kv) // HEAD_DIM
    attn = pl.pallas_call(
        _attn_kernel,
        grid=(B, N_KV_HEADS),
        in_specs=[
            pl.BlockSpec((S, gw), lambda b, h: (b, h)),
            pl.BlockSpec((S, HEAD_DIM), lambda b, h: (b, k_col0 + h)),
            pl.BlockSpec((S, HEAD_DIM), lambda b, h: (b, v_col0 + h)),
        ],
        out_specs=pl.BlockSpec((S, gw), lambda b, h: (b, h)),
        out_shape=jax.ShapeDtypeStruct((T, attn_q), BF16),
        compiler_params=_params("parallel", "parallel"),
        name="attn",
    )(qk, qk, z)

    conv = pl.pallas_call(
        _conv_kernel,
        grid=(B,),
        in_specs=[
            pl.BlockSpec((S, CONV_CH), lambda b: (b, u_col // CONV_CH)),
            pl.BlockSpec((S, CONV_CH), lambda b: (b, u_col // CONV_CH + 1)),
            pl.BlockSpec(conv_dw_w.shape[1:], lambda b: (0, 0)),
            pl.BlockSpec((1, CONV_CH), lambda b: (0, 0)),
            pl.BlockSpec((1, CONV_CH), lambda b: (0, 0)),
            pl.BlockSpec((1, CONV_CH), lambda b: (0, 0)),
        ],
        out_specs=pl.BlockSpec((S, CONV_CH), lambda b: (b, 0)),
        out_shape=jax.ShapeDtypeStruct((T, CONV_CH), BF16),
        scratch_shapes=[pltpu.VMEM((S + 2 * HALO, CONV_CH), F32),
                        pltpu.VMEM((S, CONV_CH), F32)],
        compiler_params=_params("parallel"),
        name="conv",
    )(z, z, conv_dw_w[0].astype(F32), _row(conv_dw_b[0]), _row(conv_ln_g[0]), _row(conv_ln_b[0]))

    z_blk = D
    once = pl.Buffered(1)
    x1, h2 = pl.pallas_call(
        _mixout_kernel,
        grid=(T // TM_MIX,),
        in_specs=[
            pl.BlockSpec((TM_MIX, attn_q), lambda i: (i, 0)),
            pl.BlockSpec((TM_MIX, CONV_CH), lambda i: (i, 0)),
            pl.BlockSpec((TM_MIX, z_blk), lambda i: (i, gate_col // z_blk)),
            pl.BlockSpec((TM_MIX, z_blk), lambda i: (i, gate_col // z_blk + 1)),
            pl.BlockSpec((TM_MIX, D), lambda i: (i, 0)),
            pl.BlockSpec((attn_q, D), lambda i: (0, 0), pipeline_mode=once),
            pl.BlockSpec((CONV_CH, D), lambda i: (0, 0), pipeline_mode=once),
            pl.BlockSpec((D, D), lambda i: (0, 0), pipeline_mode=once),
            pl.BlockSpec((1, D), lambda i: (0, 0)),
        ],
        out_specs=[pl.BlockSpec((TM_MIX, D), lambda i: (i, 0)),
                   pl.BlockSpec((TM_MIX, D), lambda i: (i, 0))],
        out_shape=[jax.ShapeDtypeStruct((T, D), F32), jax.ShapeDtypeStruct((T, D), BF16)],
        scratch_shapes=[pltpu.VMEM((TM_MIX, D), BF16)],
        compiler_params=_params("parallel"),
        name="mixout",
    )(attn, conv, z, z, xt, w_attn_o[0].astype(BF16), w_conv_o[0].astype(BF16),
      w_out[0].astype(BF16), _row(norm_ffn_g[0]))

    n_ft = T // TM_FFN
    hbf = TM_FFN // HALO
    n_hb = T // HALO
    n_f = ffn_dim // TF_FFN
    n_up = n_ft * n_f
    assert n_f >= 2
    w_up = w_ffn_up[0].astype(BF16)
    up_tile = lambda s: jnp.minimum(s // n_f, n_ft - 1)
    up_f = lambda s: jnp.minimum(s, n_up - 1) % n_f
    down_step = lambda s: jnp.maximum(s - 1, 0)
    y = pl.pallas_call(
        functools.partial(_ffn_kernel, n_f, n_ft, S // TM_FFN),
        grid=(n_up + 1,),
        in_specs=[
            pl.BlockSpec((HALO, D), lambda s: (jnp.maximum(up_tile(s) * hbf - 1, 0), 0)),
            pl.BlockSpec((TM_FFN, D), lambda s: (up_tile(s), 0)),
            pl.BlockSpec((HALO, D), lambda s: (jnp.minimum((up_tile(s) + 1) * hbf, n_hb - 1), 0)),
            pl.BlockSpec(memory_space=pl.ANY),
            pl.BlockSpec((D, TF_FFN), lambda s: (0, up_f(s))),
            pl.BlockSpec((D, TF_FFN), lambda s: (0, n_f + up_f(s))),
            pl.BlockSpec((ffn_dw_w.shape[1], TF_FFN), lambda s: (0, up_f(s))),
            pl.BlockSpec((1, TF_FFN), lambda s: (0, up_f(s))),
            pl.BlockSpec((TF_FFN, D), lambda s: (down_step(s) % n_f, 0)),
            pl.BlockSpec((1, D), lambda s: (0, 0)),
        ],
        out_specs=pl.BlockSpec(memory_space=pl.ANY),
        out_shape=jax.ShapeDtypeStruct((T, D), F32),
        scratch_shapes=[
            pltpu.VMEM((TM_FFN + 2 * HALO, D), BF16),
            pltpu.VMEM((TM_FFN + 2 * HALO, TF_FFN), F32),
            pltpu.VMEM((TM_FFN, TF_FFN), BF16),
            pltpu.VMEM((TM_FFN, TF_FFN), BF16),
            pltpu.VMEM((TM_FFN, D), F32),
            pltpu.VMEM((TM_FFN, D), F32),
            pltpu.SemaphoreType.DMA((2,)),
        ],
        compiler_params=_params("arbitrary"),
        name="ffn",
    )(h2, h2, h2, x1, w_up, w_up, ffn_dw_w[0].astype(F32), _row(ffn_dw_b[0]),
      w_ffn_down[0].astype(BF16), _row(norm_final_g))
    return y.reshape(B, S, D)
```

```python
import functools

import jax
import jax.numpy as jnp
from jax import lax
from jax.experimental import pallas as pl
from jax.experimental.pallas import tpu as pltpu

F32 = jnp.float32
BF16 = jnp.bfloat16

EPS = 1e-6
HEAD_DIM = 128
N_Q_HEADS = 8
N_KV_HEADS = 4
GQA_GROUP = N_Q_HEADS // N_KV_HEADS
ROPE_THETA = 10000.0
GRID_W = 64
CONV_CH = 1024

V7X_VMEM_BYTES = 64 * 1024 * 1024
VMEM_LIMIT = V7X_VMEM_BYTES - 8 * 1024 * 1024
SUBLANES = 8
LANES = 128
BF16_SUBLANES = 16

TM_PROJ = 1024
TN_IN = 1024
TM_MIX = 512
TN_PROJ = 512
PRO_CHUNKS = 4
QK_GROUP_W = 256
TQ_ATTN = 512
ATTN_EDGE_ROWS = (128, 384)
GLU_ROWS = 128
CONV_UNROLL = 8
LN_ROWS = 256
TM_FFN = 1024
TF_FFN = 512
HALO = BF16_SUBLANES


def _params(*sem):
    return pltpu.CompilerParams(dimension_semantics=sem, vmem_limit_bytes=VMEM_LIMIT)


def _rms_rows(x, g):
    ms = jnp.mean(x * x, axis=-1, keepdims=True)
    return x * lax.rsqrt(ms + EPS) * g


def _sigmoid(x):
    return 0.5 * jnp.tanh(0.5 * x) + 0.5


def _inproj_kernel(x_ref, g_ref, w_ref, b_ref, qg_ref, kg_ref, rc_ref, ra_ref, rb_ref, z_ref,
                   qk_ref, ha_ref, hb_ref, raw_ref):
    i = pl.program_id(0)
    j = pl.program_id(1)
    tm, tn = z_ref.shape
    group_w = qk_ref.shape[1]
    heads_per_group = group_w // HEAD_DIM
    n_groups = (N_Q_HEADS + N_KV_HEADS) * HEAD_DIM // group_w
    q_groups = N_Q_HEADS * HEAD_DIM // group_w
    raw_steps = -(-n_groups * group_w // tn)
    assert raw_steps + n_groups <= pl.num_programs(1) and tn % group_w == 0

    @pl.when((i == 0) & (j == 0))
    def _():
        ha_ref[...] = _rms_rows(x_ref[...], g_ref[...]).astype(BF16)
        raw_ref[...] = jnp.zeros_like(raw_ref)

    rows = tm // PRO_CHUNKS
    r0 = pl.multiple_of(jnp.clip(j - 1, 0, PRO_CHUNKS - 1) * rows, rows)
    grp = jnp.clip(j - raw_steps, 0, n_groups - 1)

    def step(h_cur, h_next):
        h_next[pl.ds(r0, rows), :] = _rms_rows(x_ref[pl.ds(r0, rows), :], g_ref[...]).astype(BF16)

        gain = jnp.where(grp < q_groups, qg_ref[...], kg_ref[...])
        raw = raw_ref[grp]
        rc, ra, rb = rc_ref[...], ra_ref[...], rb_ref[...]
        for hh in range(heads_per_group):
            sl = slice(hh * HEAD_DIM, (hh + 1) * HEAD_DIM)
            y = _rms_rows(raw[:, sl], gain)
            y = (y * rc + pltpu.roll(y, HEAD_DIM - 1, axis=1) * ra
                 + pltpu.roll(y, 1, axis=1) * rb)
            qk_ref[:, sl] = y.astype(BF16)

        z = jnp.dot(h_cur[...], w_ref[...], preferred_element_type=F32) + b_ref[...]
        z_ref[...] = z.astype(BF16)
        for js in range(raw_steps):
            @pl.when(j == js)
            def _(js=js):
                for gi in range(js * tn // group_w, min((js + 1) * tn // group_w, n_groups)):
                    c0 = gi * group_w - js * tn
                    raw_ref[gi] = z[:, c0:c0 + group_w]

    @pl.when(i % 2 == 0)
    def _():
        step(ha_ref, hb_ref)

    @pl.when(i % 2 == 1)
    def _():
        step(hb_ref, ha_ref)


def _attn_units(n_rows):
    lead = [ATTN_EDGE_ROWS[0], ATTN_EDGE_ROWS[1]]
    body = [TQ_ATTN] * ((n_rows - sum(lead)) // TQ_ATTN)
    assert sum(lead) + sum(body) == n_rows and GQA_GROUP == 2
    units = []
    for g, sizes in enumerate((lead + body, body + lead[::-1])):
        row = 0
        for size in sizes:
            units.append((g, row, size))
            row += size
    return units


def _attn_kernel(q_ref, k_ref, v_ref, o_ref):
    k = k_ref[...]
    v = v_ref[...]
    for g, row0, size in _attn_units(q_ref.shape[0]):
        sl = slice(g * HEAD_DIM, (g + 1) * HEAD_DIM)
        rows = slice(row0, row0 + size)
        s = lax.dot_general(q_ref[rows, sl], k, (((1,), (1,)), ((), ())),
                            preferred_element_type=F32)
        p = jnp.exp(s - jnp.max(s, axis=-1, keepdims=True))
        l = jnp.sum(p, axis=-1, keepdims=True)
        o = jnp.dot(p.astype(BF16), v, preferred_element_type=F32)
        o_ref[rows, sl] = (o / l).astype(BF16)


def _conv_kernel(u_ref, ug_ref, w_ref, b_ref, lg_ref, lb_ref, o_ref, buf_ref, conv_ref):
    ts = u_ref.shape[0]
    n_taps = w_ref.shape[0]
    pad = n_taps // 2
    buf_ref[0:HALO, :] = jnp.zeros((HALO, buf_ref.shape[1]), F32)
    buf_ref[HALO + ts:, :] = jnp.zeros((HALO, buf_ref.shape[1]), F32)

    def glu(r, carry):
        r0 = pl.multiple_of(r * GLU_ROWS, GLU_ROWS)
        u = u_ref[pl.ds(r0, GLU_ROWS), :].astype(F32)
        ug = ug_ref[pl.ds(r0, GLU_ROWS), :].astype(F32)
        buf_ref[pl.ds(HALO + r0, GLU_ROWS), :] = u * _sigmoid(ug)
        return carry

    lax.fori_loop(0, ts // GLU_ROWS, glu, 0)

    a_lo, a_hi = -((pad + SUBLANES - 1) // SUBLANES), pad // SUBLANES
    halo_groups = HALO // SUBLANES
    assert -a_lo <= halo_groups and a_hi + 1 <= halo_groups
    sub = lax.broadcasted_iota(jnp.int32, (SUBLANES, LANES), 0)

    for lane_group in range(buf_ref.shape[1] // LANES):
        ls = slice(lane_group * LANES, (lane_group + 1) * LANES)
        wv = [jnp.broadcast_to(w_ref[t:t + 1, ls], (SUBLANES, LANES)) for t in range(n_taps)]
        bias = jnp.broadcast_to(b_ref[:, ls], (SUBLANES, LANES))

        def load_group(gi, ls=ls):
            return buf_ref[pl.ds(pl.multiple_of(gi * SUBLANES, SUBLANES), SUBLANES), ls]

        def partial_sums(xs, wv=wv):
            ys = []
            for b in range(SUBLANES):
                y = None
                for ai, a in enumerate(range(a_lo, a_hi + 1)):
                    t = pad + SUBLANES * a + b
                    if 0 <= t < n_taps:
                        term = xs[ai] * wv[t]
                        y = term if y is None else y + term
                ys.append(y)
            return ys

        def body(g, carry, ls=ls, bias=bias, load_group=load_group, partial_sums=partial_sums):
            y_prev = carry[:SUBLANES]
            xs = list(carry[SUBLANES:]) + [load_group(g + 1 + a_hi + halo_groups)]
            y_new = partial_sums(xs)
            r = None
            for b in range(SUBLANES - 1, -1, -1):
                cb = y_prev[0] if b == 0 else jnp.where(sub >= b, y_prev[b], y_new[b])
                r = cb if r is None else cb + pltpu.roll(r, SUBLANES - 1, axis=0)
            conv_ref[pl.ds(pl.multiple_of(g * SUBLANES, SUBLANES), SUBLANES), ls] = r + bias
            return tuple(y_new) + tuple(xs[1:])

        x0 = [load_group(halo_groups + a_lo + i) for i in range(a_hi - a_lo + 1)]
        lax.fori_loop(0, ts // SUBLANES, body, tuple(partial_sums(x0)) + tuple(x0[1:]),
                      unroll=CONV_UNROLL)

    def layer_norm(r, carry):
        r0 = pl.multiple_of(r * LN_ROWS, LN_ROWS)
        acc = conv_ref[pl.ds(r0, LN_ROWS), :]
        mu = jnp.mean(acc, axis=-1, keepdims=True)
        d = acc - mu
        var = jnp.mean(d * d, axis=-1, keepdims=True)
        y = d * lax.rsqrt(var + EPS) * lg_ref[...] + lb_ref[...]
        o_ref[pl.ds(r0, LN_ROWS), :] = (y * _sigmoid(y)).astype(BF16)
        return carry

    lax.fori_loop(0, ts // LN_ROWS, layer_norm, 0)


def _mixout_kernel(a_ref, c_ref, ga_ref, gc_ref, x_ref, wa_ref, wc_ref, wo_ref, fg_ref,
                   x1_ref, h2_ref, m_ref):
    a = a_ref[...]
    c = c_ref[...]
    n_chunks = x_ref.shape[1] // TN_PROJ
    for jc in range(n_chunks):
        cs = slice(jc * TN_PROJ, (jc + 1) * TN_PROJ)
        pa = jnp.dot(a, wa_ref[:, cs], preferred_element_type=F32)
        pc = jnp.dot(c, wc_ref[:, cs], preferred_element_type=F32)
        ga = _sigmoid(ga_ref[:, cs].astype(F32))
        gc = _sigmoid(gc_ref[:, cs].astype(F32))
        m_ref[:, cs] = (ga * pa + gc * pc).astype(BF16)
    m = m_ref[...]
    for jc in range(n_chunks):
        cs = slice(jc * TN_PROJ, (jc + 1) * TN_PROJ)
        x1_ref[:, cs] = x_ref[:, cs] + jnp.dot(m, wo_ref[:, cs], preferred_element_type=F32)
    h2_ref[...] = _rms_rows(x1_ref[...], fg_ref[...]).astype(BF16)


def _ffn_kernel(n_f, n_tiles, n_seq_tiles, prev_ref, cur_ref, next_ref, x1_hbm, wg_ref, wv_ref,
                cw_ref, cb_ref, wd_ref, fg_ref, out_hbm, h_ref, gt_ref, acta_ref, actb_ref, acc_ref,
                y_ref, sem_ref):
    s = pl.program_id(0)
    n_up = n_f * n_tiles
    f_up = s % n_f
    tile_up = jnp.minimum(s // n_f, n_tiles - 1)
    tile_dn = jnp.maximum(s - 1, 0) // n_f
    si = tile_up % n_seq_tiles
    tm = cur_ref.shape[0]

    def tile_rows(ref, tile):
        return ref.at[pl.ds(pl.multiple_of(tile * tm, tm), tm), :]

    def res_copy(tile):
        return pltpu.make_async_copy(tile_rows(x1_hbm, tile), acc_ref, sem_ref.at[0])

    def out_copy(tile):
        return pltpu.make_async_copy(y_ref, tile_rows(out_hbm, tile), sem_ref.at[1])

    @pl.when(s == 0)
    def _():
        res_copy(0).start()

    @pl.when((f_up == 1) & (s < n_up))
    def _():
        res_copy(tile_up).wait()

    @pl.when((f_up == 1) & (s > n_f) & (s < n_up))
    def _():
        out_copy(tile_up - 1).wait()

    def load_tile():
        prev = prev_ref[...]
        nxt = next_ref[...]
        h_ref[0:HALO, :] = jnp.where(si > 0, prev, jnp.zeros_like(prev))
        h_ref[HALO:HALO + tm, :] = cur_ref[...]
        h_ref[HALO + tm:, :] = jnp.where(si < n_seq_tiles - 1, nxt, jnp.zeros_like(nxt))

    def up(act_ref):
        gt_ref[...] = jnp.dot(h_ref[...], wg_ref[...], preferred_element_type=F32)
        val = jnp.dot(h_ref[HALO:HALO + tm, :], wv_ref[...], preferred_element_type=F32)
        gt = (gt_ref[HALO - 1:HALO - 1 + tm, :] * cw_ref[0:1, :]
              + gt_ref[HALO:HALO + tm, :] * cw_ref[1:2, :]
              + gt_ref[HALO + 1:HALO + 1 + tm, :] * cw_ref[2:3, :]
              + cb_ref[...])
        act_ref[...] = (gt * _sigmoid(gt) * val).astype(BF16)

    def down(act_ref):
        acc_ref[...] += jnp.dot(act_ref[...], wd_ref[...], preferred_element_type=F32)

    def finish_tile():
        y_ref[...] = _rms_rows(acc_ref[...], fg_ref[...])
        out_copy(tile_dn).start()

    acts = (acta_ref, actb_ref)
    for parity in (0, 1):
        a_new, a_old = acts[parity], acts[1 - parity]
        mine = (s % 2) == parity

        @pl.when(mine & (s == 0))
        def _():
            load_tile()
            up(a_new)

        @pl.when(mine & (f_up == 0) & (s > 0) & (s < n_up))
        def _():
            down(a_old)
            finish_tile()
            res_copy(tile_up).start()
            load_tile()
            up(a_new)

        @pl.when(mine & (f_up >= 1))
        def _():
            up(a_new)
            down(a_old)

        @pl.when(mine & (s == n_up))
        def _():
            down(a_old)
            finish_tile()
            out_copy(tile_dn).wait()


def _rope_tables(seq_len):
    rows = seq_len // GRID_W
    half = HEAD_DIM // 2
    row = jnp.repeat(jnp.arange(rows), GRID_W).astype(F32)
    col = jnp.tile(jnp.arange(GRID_W), rows).astype(F32)
    inv_freq = ROPE_THETA ** (-jnp.arange(0, half, 2, dtype=F32) / half)
    ang = jnp.concatenate([row[:, None] * inv_freq, col[:, None] * inv_freq], axis=-1)
    cos = jnp.repeat(jnp.cos(ang), 2, axis=-1)
    sin = jnp.repeat(jnp.sin(ang), 2, axis=-1)
    even = (jnp.arange(HEAD_DIM) % 2 == 0)[None, :]
    return cos, jnp.where(even, -sin, 0.0), jnp.where(even, 0.0, sin)


def _row(v):
    return v.reshape(1, -1).astype(F32)


def kernel(x, norm_mix_g, w_in, b_in, q_norm_g, k_norm_g, w_attn_o, conv_dw_w, conv_dw_b,
           conv_ln_g, conv_ln_b, w_conv_o, w_out, norm_ffn_g, w_ffn_up, ffn_dw_w, ffn_dw_b,
           w_ffn_down, norm_final_g):
    B, S, D = x.shape
    T = B * S
    assert w_in.shape[0] == 1, "single trunk layer: the final rmsnorm is fused into the FFN"
    attn_q = N_Q_HEADS * HEAD_DIM
    attn_kv = N_KV_HEADS * HEAD_DIM
    in_cols = w_in.shape[2]
    ffn_dim = w_ffn_down.shape[1]
    u_col = attn_q + 2 * attn_kv
    gate_col = u_col + 2 * CONV_CH
    assert in_cols == gate_col + 2 * D
    assert T % TM_PROJ == 0 and S % TM_PROJ == 0 and in_cols % TN_IN == 0
    assert attn_q % TN_IN == 0 and u_col % CONV_CH == 0 and D % TN_PROJ == 0
    assert S % TQ_ATTN == 0 and S % TM_FFN == 0 and ffn_dim % TF_FFN == 0 and T % TM_MIX == 0

    rope_c, rope_a, rope_b = _rope_tables(S)
    xt = x.reshape(T, D)
    n_i = T // TM_PROJ
    seq_tiles = S // TM_PROJ

    const2 = lambda i, j: (0, 0)
    rope_spec = pl.BlockSpec((TM_PROJ, HEAD_DIM), lambda i, j: (i % seq_tiles, 0))
    n_j = in_cols // TN_IN
    qk_cols = attn_q + attn_kv
    n_groups = qk_cols // QK_GROUP_W
    raw_steps = -(-qk_cols // TN_IN)
    z, qk = pl.pallas_call(
        _inproj_kernel,
        grid=(n_i, n_j),
        in_specs=[
            pl.BlockSpec((TM_PROJ, D), lambda i, j: (
                jnp.where((i == 0) & (j == 0), 0, jnp.minimum(i + 1, n_i - 1)), 0)),
            pl.BlockSpec((1, D), const2),
            pl.BlockSpec((D, TN_IN), lambda i, j: (0, j)),
            pl.BlockSpec((1, TN_IN), lambda i, j: (0, j)),
            pl.BlockSpec((1, HEAD_DIM), const2),
            pl.BlockSpec((1, HEAD_DIM), const2),
            rope_spec, rope_spec, rope_spec,
        ],
        out_specs=[
            pl.BlockSpec((TM_PROJ, TN_IN), lambda i, j: (i, j)),
            pl.BlockSpec((TM_PROJ, QK_GROUP_W),
                         lambda i, j: (i, jnp.clip(j - raw_steps, 0, n_groups - 1))),
        ],
        out_shape=[jax.ShapeDtypeStruct((T, in_cols), BF16),
                   jax.ShapeDtypeStruct((T, qk_cols), BF16)],
        scratch_shapes=[pltpu.VMEM((TM_PROJ, D), BF16), pltpu.VMEM((TM_PROJ, D), BF16),
                        pltpu.VMEM((n_groups, TM_PROJ, QK_GROUP_W), F32)],
        compiler_params=_params("arbitrary", "arbitrary"),
        name="inproj",
    )(xt, _row(norm_mix_g[0]), w_in[0].astype(BF16), _row(b_in[0]),
      _row(q_norm_g[0]) * HEAD_DIM ** -0.5, _row(k_norm_g[0]), rope_c, rope_a, rope_b)

    gw = GQA_GROUP * HEAD_DIM
    k_col0 = attn_q // HEAD_DIM
    v_col0 = (attn_q + attn_kv) // HEAD_DIM
    attn = pl.pallas_call(
        _attn_kernel,
        grid=(B, N_KV_HEADS),
        in_specs=[
            pl.BlockSpec((S, gw), lambda b, h: (b, h)),
            pl.BlockSpec((S, HEAD_DIM), lambda b, h: (b, k_col0 + h)),
            pl.BlockSpec((S, HEAD_DIM), lambda b, h: (b, v_col0 + h)),
        ],
        out_specs=pl.BlockSpec((S, gw), lambda b, h: (b, h)),
        out_shape=jax.ShapeDtypeStruct((T, attn_q), BF16),
        compiler_params=_params("parallel", "parallel"),
        name="attn",
    )(qk, qk, z)

    conv = pl.pallas_call(
        _conv_kernel,
        grid=(B,),
        in_specs=[
            pl.BlockSpec((S, CONV_CH), lambda b: (b, u_col // CONV_CH)),
            pl.BlockSpec((S, CONV_CH), lambda b: (b, u_col // CONV_CH + 1)),
            pl.BlockSpec(conv_dw_w.shape[1:], lambda b: (0, 0)),
            pl.BlockSpec((1, CONV_CH), lambda b: (0, 0)),
            pl.BlockSpec((1, CONV_CH), lambda b: (0, 0)),
            pl.BlockSpec((1, CONV_CH), lambda b: (0, 0)),
        ],
        out_specs=pl.BlockSpec((S, CONV_CH), lambda b: (b, 0)),
        out_shape=jax.ShapeDtypeStruct((T, CONV_CH), BF16),
        scratch_shapes=[pltpu.VMEM((S + 2 * HALO, CONV_CH), F32),
                        pltpu.VMEM((S, CONV_CH), F32)],
        compiler_params=_params("parallel"),
        name="conv",
    )(z, z, conv_dw_w[0].astype(F32), _row(conv_dw_b[0]), _row(conv_ln_g[0]), _row(conv_ln_b[0]))

    z_blk = D
    once = pl.Buffered(1)
    x1, h2 = pl.pallas_call(
        _mixout_kernel,
        grid=(T // TM_MIX,),
        in_specs=[
            pl.BlockSpec((TM_MIX, attn_q), lambda i: (i, 0)),
            pl.BlockSpec((TM_MIX, CONV_CH), lambda i: (i, 0)),
            pl.BlockSpec((TM_MIX, z_blk), lambda i: (i, gate_col // z_blk)),
            pl.BlockSpec((TM_MIX, z_blk), lambda i: (i, gate_col // z_blk + 1)),
            pl.BlockSpec((TM_MIX, D), lambda i: (i, 0)),
            pl.BlockSpec((attn_q, D), lambda i: (0, 0), pipeline_mode=once),
            pl.BlockSpec((CONV_CH, D), lambda i: (0, 0), pipeline_mode=once),
            pl.BlockSpec((D, D), lambda i: (0, 0), pipeline_mode=once),
            pl.BlockSpec((1, D), lambda i: (0, 0)),
        ],
        out_specs=[pl.BlockSpec((TM_MIX, D), lambda i: (i, 0)),
                   pl.BlockSpec((TM_MIX, D), lambda i: (i, 0))],
        out_shape=[jax.ShapeDtypeStruct((T, D), F32), jax.ShapeDtypeStruct((T, D), BF16)],
        scratch_shapes=[pltpu.VMEM((TM_MIX, D), BF16)],
        compiler_params=_params("parallel"),
        name="mixout",
    )(attn, conv, z, z, xt, w_attn_o[0].astype(BF16), w_conv_o[0].astype(BF16),
      w_out[0].astype(BF16), _row(norm_ffn_g[0]))

    n_ft = T // TM_FFN
    hbf = TM_FFN // HALO
    n_hb = T // HALO
    n_f = ffn_dim // TF_FFN
    n_up = n_ft * n_f
    assert n_f >= 2
    w_up = w_ffn_up[0].astype(BF16)
    up_tile = lambda s: jnp.minimum(s // n_f, n_ft - 1)
    up_f = lambda s: jnp.minimum(s, n_up - 1) % n_f
    down_step = lambda s: jnp.maximum(s - 1, 0)
    y = pl.pallas_call(
        functools.partial(_ffn_kernel, n_f, n_ft, S // TM_FFN),
        grid=(n_up + 1,),
        in_specs=[
            pl.BlockSpec((HALO, D), lambda s: (jnp.maximum(up_tile(s) * hbf - 1, 0), 0)),
            pl.BlockSpec((TM_FFN, D), lambda s: (up_tile(s), 0)),
            pl.BlockSpec((HALO, D), lambda s: (jnp.minimum((up_tile(s) + 1) * hbf, n_hb - 1), 0)),
            pl.BlockSpec(memory_space=pl.ANY),
            pl.BlockSpec((D, TF_FFN), lambda s: (0, up_f(s))),
            pl.BlockSpec((D, TF_FFN), lambda s: (0, n_f + up_f(s))),
            pl.BlockSpec((ffn_dw_w.shape[1], TF_FFN), lambda s: (0, up_f(s))),
            pl.BlockSpec((1, TF_FFN), lambda s: (0, up_f(s))),
            pl.BlockSpec((TF_FFN, D), lambda s: (down_step(s) % n_f, 0)),
            pl.BlockSpec((1, D), lambda s: (0, 0)),
        ],
        out_specs=pl.BlockSpec(memory_space=pl.ANY),
        out_shape=jax.ShapeDtypeStruct((T, D), F32),
        scratch_shapes=[
            pltpu.VMEM((TM_FFN + 2 * HALO, D), BF16),
            pltpu.VMEM((TM_FFN + 2 * HALO, TF_FFN), F32),
            pltpu.VMEM((TM_FFN, TF_FFN), BF16),
            pltpu.VMEM((TM_FFN, TF_FFN), BF16),
            pltpu.VMEM((TM_FFN, D), F32),
            pltpu.VMEM((TM_FFN, D), F32),
            pltpu.SemaphoreType.DMA((2,)),
        ],
        compiler_params=_params("arbitrary"),
        name="ffn",
    )(h2, h2, h2, x1, w_up, w_up, ffn_dw_w[0].astype(F32), _row(ffn_dw_b[0]),
      w_ffn_down[0].astype(BF16), _row(norm_final_g))
    return y.reshape(B, S, D)
```
